```python
import math
import jax
import jax.numpy as jnp
from jax import lax
import numpy as np

D_MODEL = 1024
BATCH = 8
SEQ = 2048
DEPTH = 2

PLE_DIM = 256
N_EVEN = (DEPTH + 1) // 2
N_ODD = DEPTH // 2
SSD_HEADS = 16
SSD_HEAD_DIM = 64
SSD_INNER = SSD_HEADS * SSD_HEAD_DIM
SSD_GROUPS = 2
SSD_HPG = SSD_HEADS // SSD_GROUPS
SSD_STATE = 128
SSD_CONV = 4
SSD_CHUNK = 128
SSD_XBC = SSD_INNER + 2 * SSD_GROUPS * SSD_STATE
SC_DIM = 1024
SC_CONV = 3
MIX_IN = SSD_INNER + SSD_XBC + SSD_HEADS + 3 * SC_DIM
MIX_OUT = SSD_INNER + SC_DIM
ATTN_HEADS = 16
ATTN_KV_HEADS = 4
ATTN_GROUP = ATTN_HEADS // ATTN_KV_HEADS
ATTN_HEAD_DIM = 64
WINDOW = 128
ATTN_BLOCK = 128
QKV_DIM = (ATTN_HEADS + 2 * ATTN_KV_HEADS) * ATTN_HEAD_DIM
D_FF = ((8 * D_MODEL + 3 * 256 - 1) // (3 * 256)) * 256
ALPHA = (2 * DEPTH) ** 0.25
BETA = (8 * DEPTH) ** -0.25
LN_EPS = 1e-5
RMS_EPS = 1e-5

kernel_name = 'hybrid_ssd_shortconv_swa_deepnorm'


def layer_norm(x, g, b):
    xf = x.astype(jnp.float32)
    mu = jnp.mean(xf, axis=-1, keepdims=True)
    xc = xf - mu
    var = jnp.mean(xc * xc, axis=-1, keepdims=True)
    return (xc * lax.rsqrt(var + LN_EPS) * g.astype(jnp.float32) + b.astype(jnp.float32)).astype(x.dtype)


def causal_dwconv(x, w):
    k, c = w.shape
    return lax.conv_general_dilated(
        x, w[:, None, :].astype(x.dtype), window_strides=(1,), padding=[(k - 1, 0)],
        dimension_numbers=('NWC', 'WIO', 'NWC'), feature_group_count=c)


def ssd_mixer(z, xbc, dt_raw, conv_w, conv_b, dt_bias, a_log, d_skip, norm_w):
    f32 = jnp.float32
    bsz, seqlen, _ = z.shape
    nc = seqlen // SSD_CHUNK
    xbc = jax.nn.silu(causal_dwconv(xbc, conv_w) + conv_b.astype(xbc.dtype)).astype(f32)
    xs = xbc[..., :SSD_INNER].reshape(bsz, nc, SSD_CHUNK, SSD_GROUPS, SSD_HPG, SSD_HEAD_DIM)
    bm = xbc[..., SSD_INNER:SSD_INNER + SSD_GROUPS * SSD_STATE].reshape(bsz, nc, SSD_CHUNK, SSD_GROUPS, SSD_STATE)
    cm = xbc[..., SSD_INNER + SSD_GROUPS * SSD_STATE:].reshape(bsz, nc, SSD_CHUNK, SSD_GROUPS, SSD_STATE)
    dt = jax.nn.softplus(dt_raw.astype(f32) + dt_bias.astype(f32))
    a = -jnp.exp(a_log.astype(f32))
    dt = dt.reshape(bsz, nc, SSD_CHUNK, SSD_GROUPS, SSD_HPG)
    da = dt * a.reshape(SSD_GROUPS, SSD_HPG)
    xdt = xs * dt[..., None]
    a_cs = jnp.cumsum(da, axis=2)
    seg = a_cs[:, :, :, None] - a_cs[:, :, None, :]
    causal = jnp.tril(jnp.ones((SSD_CHUNK, SSD_CHUNK), dtype=bool))[:, :, None, None]
    decay_ls = jnp.exp(jnp.where(causal, seg, -jnp.inf))
    cb = jnp.einsum('bclgn,bcsgn->bclsg', cm, bm)
    y_diag = jnp.einsum('bclsgr,bcsgrp->bclgrp', cb[..., None] * decay_ls, xdt)
    decay_to_end = jnp.exp(a_cs[:, :, -1:] - a_cs)
    chunk_states = jnp.einsum('bclgn,bclgrp->bcgrpn', bm, xdt * decay_to_end[..., None])
    chunk_decay = jnp.exp(a_cs[:, :, -1])

    def step(h, inp):
        s_c, d_c = inp
        return h * d_c[..., None, None] + s_c, h

    h0 = jnp.zeros((bsz, SSD_GROUPS, SSD_HPG, SSD_HEAD_DIM, SSD_STATE), f32)
    _, h_in = lax.scan(step, h0, (jnp.moveaxis(chunk_states, 1, 0), jnp.moveaxis(chunk_decay, 1, 0)))
    h_in = jnp.moveaxis(h_in, 0, 1)
    y_off = jnp.einsum('bclgn,bcgrpn->bclgrp', cm, h_in) * jnp.exp(a_cs)[..., None]
    y = y_diag + y_off + xs * d_skip.astype(f32).reshape(SSD_GROUPS, SSD_HPG)[:, :, None]
    y = y.reshape(bsz, seqlen, SSD_INNER) * jax.nn.silu(z.astype(f32))
    yg = y.reshape(bsz, seqlen, SSD_GROUPS, SSD_INNER // SSD_GROUPS)
    yg = yg * lax.rsqrt(jnp.mean(yg * yg, axis=-1, keepdims=True) + RMS_EPS)
    return (yg.reshape(bsz, seqlen, SSD_INNER) * norm_w.astype(f32)).astype(z.dtype)


def ssd_shortconv_layer_mixer(x, w_in, conv_w, conv_b, dt_bias, a_log, d_skip, norm_w, sc_conv_w, w_out):
    proj = x @ w_in
    o1 = SSD_INNER
    o2 = o1 + SSD_XBC
    o3 = o2 + SSD_HEADS
    o4 = o3 + SC_DIM
    o5 = o4 + SC_DIM
    z, xbc, dt_raw = proj[..., :o1], proj[..., o1:o2], proj[..., o2:o3]
    sc_b, sc_c, sc_h = proj[..., o3:o4], proj[..., o4:o5], proj[..., o5:]
    y_ssd = ssd_mixer(z, xbc, dt_raw, conv_w, conv_b, dt_bias, a_log, d_skip, norm_w)
    y_sc = sc_b * causal_dwconv(sc_c * sc_h, sc_conv_w)
    return jnp.concatenate([y_ssd.astype(x.dtype), y_sc.astype(x.dtype)], axis=-1) @ w_out


def sliding_window_attention(x, w_qkv, b_qkv, sinks, w_o, b_o):
    bsz, seqlen, _ = x.shape
    nb = seqlen // ATTN_BLOCK
    qkv = x @ w_qkv + b_qkv
    nq = ATTN_HEADS * ATTN_HEAD_DIM
    nkv = ATTN_KV_HEADS * ATTN_HEAD_DIM
    q = qkv[..., :nq].reshape(bsz, nb, ATTN_BLOCK, ATTN_KV_HEADS, ATTN_GROUP, ATTN_HEAD_DIM)
    k = qkv[..., nq:nq + nkv].reshape(bsz, nb, ATTN_BLOCK, ATTN_KV_HEADS, ATTN_HEAD_DIM)
    v = qkv[..., nq + nkv:].reshape(bsz, nb, ATTN_BLOCK, ATTN_KV_HEADS, ATTN_HEAD_DIM)

    def with_prev(t):
        prev = jnp.concatenate([jnp.zeros_like(t[:, :1]), t[:, :-1]], axis=1)
        return jnp.concatenate([prev, t], axis=2)

    kk = with_prev(k)
    vv = with_prev(v)
    scale = ATTN_HEAD_DIM ** -0.5
    s = jnp.einsum('bnqkgd,bnskd->bnkgqs', q, kk, preferred_element_type=jnp.float32) * scale
    n_idx = jnp.arange(nb)[:, None, None]
    q_pos = n_idx * ATTN_BLOCK + jnp.arange(ATTN_BLOCK)[None, :, None]
    k_pos = (n_idx - 1) * ATTN_BLOCK + jnp.arange(2 * ATTN_BLOCK)[None, None, :]
    rel = q_pos - k_pos
    band = (rel >= 0) & (rel < WINDOW) & (k_pos >= 0)
    s = jnp.where(band[None, :, None, None], s, -jnp.inf)
    sink = sinks.astype(jnp.float32).reshape(ATTN_KV_HEADS, ATTN_GROUP)[None, None, :, :, None, None]
    sink = jnp.broadcast_to(sink, s.shape[:-1] + (1,))
    probs = jax.nn.softmax(jnp.concatenate([s, sink], axis=-1), axis=-1)[..., :-1]
    o = jnp.einsum('bnkgqs,bnskd->bnqkgd', probs.astype(vv.dtype), vv)
    return o.reshape(bsz, seqlen, nq) @ w_o + b_o


def swiglu(x, w_gate, w_up, w_down):
    return (jax.nn.silu(x @ w_gate) * (x @ w_up)) @ w_down


def setup_inputs(seed: int = 0) -> dict:
    key = jax.random.key(seed)
    ks = jax.random.split(key, 32)
    f32 = jnp.float32

    def nrm(k, shape, scale):
        return jax.random.normal(k, shape, f32) * scale

    x = nrm(ks[0], (BATCH, SEQ, D_MODEL), 1.0)
    p = nrm(ks[1], (DEPTH, BATCH, SEQ, PLE_DIM), 1.0)
    w_in_mix = nrm(ks[2], (N_EVEN, D_MODEL, MIX_IN), D_MODEL ** -0.5)
    ssd_conv_w = nrm(ks[3], (N_EVEN, SSD_CONV, SSD_XBC), SSD_CONV ** -0.5)
    ssd_conv_b = nrm(ks[4], (N_EVEN, SSD_XBC), 0.02)
    dt0 = jnp.exp(jax.random.uniform(ks[5], (N_EVEN, SSD_HEADS), f32, math.log(1e-3), math.log(1e-1)))
    ssd_dt_bias = dt0 + jnp.log(-jnp.expm1(-dt0))
    ssd_a_log = jnp.log(jax.random.uniform(ks[6], (N_EVEN, SSD_HEADS), f32, 1.0, 16.0))
    ssd_d = 1.0 + nrm(ks[7], (N_EVEN, SSD_HEADS), 0.1)
    ssd_norm_w = 1.0 + nrm(ks[8], (N_EVEN, SSD_INNER), 0.02)
    sc_conv_w = nrm(ks[9], (N_EVEN, SC_CONV, SC_DIM), SC_CONV ** -0.5)
    w_out_mix = nrm(ks[10], (N_EVEN, MIX_OUT, D_MODEL), BETA * MIX_OUT ** -0.5)
    w_qkv = nrm(ks[11], (N_ODD, D_MODEL, QKV_DIM), D_MODEL ** -0.5)
    b_qkv = nrm(ks[12], (N_ODD, QKV_DIM), 0.02)
    attn_sinks = nrm(ks[13], (N_ODD, ATTN_HEADS), 0.5)
    w_o = nrm(ks[14], (N_ODD, ATTN_HEADS * ATTN_HEAD_DIM, D_MODEL), BETA * (ATTN_HEADS * ATTN_HEAD_DIM) ** -0.5)
    b_o = nrm(ks[15], (N_ODD, D_MODEL), 0.02)
    ln_mix_g = 1.0 + nrm(ks[16], (DEPTH, D_MODEL), 0.02)
    ln_mix_b = nrm(ks[17], (DEPTH, D_MODEL), 0.02)
    w_ffn_gate = nrm(ks[18], (DEPTH, D_MODEL, D_FF), D_MODEL ** -0.5)
    w_ffn_up = nrm(ks[19], (DEPTH, D_MODEL, D_FF), D_MODEL ** -0.5)
    w_ffn_down = nrm(ks[20], (DEPTH, D_FF, D_MODEL), BETA * D_FF ** -0.5)
    ln_ffn_g = 1.0 + nrm(ks[21], (DEPTH, D_MODEL), 0.02)
    ln_ffn_b = nrm(ks[22], (DEPTH, D_MODEL), 0.02)
    w_ple = nrm(ks[23], (DEPTH, PLE_DIM, D_MODEL), PLE_DIM ** -0.5)
    w_ple_gate = nrm(ks[24], (DEPTH, D_MODEL, D_MODEL), D_MODEL ** -0.5)
    b_ple_gate = nrm(ks[25], (DEPTH, D_MODEL), 0.02)
    return {'x': x, 'p': p, 'w_in_mix': w_in_mix, 'ssd_conv_w': ssd_conv_w, 'ssd_conv_b': ssd_conv_b,
            'ssd_dt_bias': ssd_dt_bias, 'ssd_a_log': ssd_a_log, 'ssd_d': ssd_d, 'ssd_norm_w': ssd_norm_w,
            'sc_conv_w': sc_conv_w, 'w_out_mix': w_out_mix, 'w_qkv': w_qkv, 'b_qkv': b_qkv,
            'attn_sinks': attn_sinks, 'w_o': w_o, 'b_o': b_o, 'ln_mix_g': ln_mix_g, 'ln_mix_b': ln_mix_b,
            'w_ffn_gate': w_ffn_gate, 'w_ffn_up': w_ffn_up, 'w_ffn_down': w_ffn_down,
            'ln_ffn_g': ln_ffn_g, 'ln_ffn_b': ln_ffn_b, 'w_ple': w_ple, 'w_ple_gate': w_ple_gate,
            'b_ple_gate': b_ple_gate}


def reference(x, p, w_in_mix, ssd_conv_w, ssd_conv_b, ssd_dt_bias, ssd_a_log, ssd_d, ssd_norm_w,
              sc_conv_w, w_out_mix, w_qkv, b_qkv, attn_sinks, w_o, b_o, ln_mix_g, ln_mix_b,
              w_ffn_gate, w_ffn_up, w_ffn_down, ln_ffn_g, ln_ffn_b, w_ple, w_ple_gate, b_ple_gate):
    for i in range(DEPTH):
        j = i // 2
        if i % 2 == 0:
            mix = ssd_shortconv_layer_mixer(x, w_in_mix[j], ssd_conv_w[j], ssd_conv_b[j], ssd_dt_bias[j],
                                            ssd_a_log[j], ssd_d[j], ssd_norm_w[j], sc_conv_w[j], w_out_mix[j])
        else:
            mix = sliding_window_attention(x, w_qkv[j], b_qkv[j], attn_sinks[j], w_o[j], b_o[j])
        x = layer_norm(ALPHA * x + mix.astype(x.dtype), ln_mix_g[i], ln_mix_b[i])
        x = layer_norm(ALPHA * x + swiglu(x, w_ffn_gate[i], w_ffn_up[i], w_ffn_down[i]), ln_ffn_g[i], ln_ffn_b[i])
        gate = jax.nn.sigmoid(x @ w_ple_gate[i] + b_ple_gate[i])
        x = x + (p[i] @ w_ple[i]) * gate
    return x
```

```python
import functools

import jax
import jax.numpy as jnp
from jax import lax
from jax.experimental import pallas as pl
from jax.experimental.pallas import tpu as pltpu

F32 = jnp.float32
BF16 = jnp.bfloat16

D_MODEL = 1024
DEPTH = 2
PLE_DIM = 256
SSD_HEADS = 16
SSD_HEAD_DIM = 64
SSD_INNER = SSD_HEADS * SSD_HEAD_DIM
SSD_GROUPS = 2
SSD_HPG = SSD_HEADS // SSD_GROUPS
SSD_STATE = 128
SSD_CONV = 4
SSD_CHUNK = 128
SSD_XBC = SSD_INNER + 2 * SSD_GROUPS * SSD_STATE
SC_DIM = 1024
SC_CONV = 3
MIX_OUT = SSD_INNER + SC_DIM
ATTN_HEADS = 16
ATTN_KV_HEADS = 4
ATTN_HEAD_DIM = 64
ATTN_BLOCK = 128
D_FF = ((8 * D_MODEL + 3 * 256 - 1) // (3 * 256)) * 256
ALPHA = (2 * DEPTH) ** 0.25
LN_EPS = 1e-5
RMS_EPS = 1e-5

LANES = 128
SUBLANES = 8
V7X_VMEM_BYTES = 64 * 1024 * 1024
VMEM_LIMIT_BYTES = V7X_VMEM_BYTES - 8 * 1024 * 1024

TM_MIX0 = 512
TM_ATTN = 512
TM_FFN = 512

_C_Z = 0
_C_XBC = _C_Z + SSD_INNER
_C_SCB = _C_XBC + SSD_XBC
_C_SCC = _C_SCB + SC_DIM
_C_SCH = _C_SCC + SC_DIM
_C_DT = _C_SCH + SC_DIM
MIX_IN_P = _C_DT + LANES
DT_REP = 3


def _dot(a, b):
    return jnp.dot(a, b, preferred_element_type=F32)


def _dot_nt(a, b):
    return lax.dot_general(a, b, (((1,), (1,)), ((), ())), preferred_element_type=F32)


def _dot_tn(a, b):
    return lax.dot_general(a, b, (((0,), (0,)), ((), ())), preferred_element_type=F32)


def _sigmoid(x):
    return 1.0 / (1.0 + jnp.exp(-x))


def _silu(x):
    return x * _sigmoid(x)


def _layer_norm(r, g, b):
    mu = jnp.mean(r, axis=-1, keepdims=True)
    xc = r - mu
    var = jnp.mean(xc * xc, axis=-1, keepdims=True)
    return xc * lax.rsqrt(var + LN_EPS) * g + b


def _split_bf16(v, parts):
    out = []
    rem = v
    for i in range(parts):
        p = rem.astype(BF16)
        out.append(p)
        if i + 1 < parts:
            rem = rem - p.astype(F32)
    return out


def _causal_conv(cur, prev_tail, w_ref, width):
    cols = cur.shape[1]
    rows = lax.broadcasted_iota(jnp.int32, (SUBLANES, cols), 0)
    acc = cur * w_ref[width - 1:width, :]
    for s in range(1, width):
        rolled = pltpu.roll(cur, s, 0)
        head = jnp.where(rows < s, pltpu.roll(prev_tail, s, 0), rolled[0:SUBLANES])
        shifted = jnp.concatenate([head, rolled[SUBLANES:]], axis=0)
        acc = acc + shifted * w_ref[width - 1 - s:width - s, :]
    return acc


def _mixer0_kernel(x_ref, w_in_ref, cw_ref, cb_ref, dtb_ref, alog_ref, dexp_ref, nw_ref, scw_ref, w_out_ref,
                   g_ref, b_ref, e3_ref, f_ref, tril_ref, o_ref,
                   z_s, xbc_s, dt_s, ycat_s, prev_xbc_s, prev_sc_s, h_s):
    tm = x_ref.shape[0]

    @pl.when(pl.program_id(1) == 0)
    def _():
        prev_xbc_s[...] = jnp.zeros_like(prev_xbc_s)
        prev_sc_s[...] = jnp.zeros_like(prev_sc_s)
        h_s[...] = jnp.zeros_like(h_s)

    x = x_ref[...]
    xb = x.astype(BF16)

    def proj(lo, hi):
        return _dot(xb, w_in_ref[:, lo:hi])

    z_s[...] = proj(_C_Z, _C_XBC)

    xbc_raw = proj(_C_XBC, _C_SCB)
    xbc_s[...] = _silu(_causal_conv(xbc_raw, prev_xbc_s[...], cw_ref, SSD_CONV) + cb_ref[...])
    prev_xbc_s[...] = xbc_raw[tm - SUBLANES:tm]

    lane_row = lax.broadcasted_iota(jnp.int32, (1, LANES), 1)
    dt_pre = proj(_C_DT, MIX_IN_P) + dtb_ref[...]
    dt_s[...] = jnp.maximum(dt_pre, 0.0) + jnp.log1p(jnp.exp(-jnp.abs(dt_pre)))
    a_row = jnp.where(lane_row < DT_REP * SSD_HEADS, -jnp.exp(alog_ref[...]), 0.0)

    u = proj(_C_SCC, _C_SCH) * proj(_C_SCH, _C_DT)
    y_sc = proj(_C_SCB, _C_SCC) * _causal_conv(u, prev_sc_s[...], scw_ref, SC_CONV)
    ycat_s[:, SSD_INNER:MIX_OUT] = y_sc.astype(BF16)
    prev_sc_s[...] = u[tm - SUBLANES:tm]

    c = SSD_CHUNK
    gw = SSD_HPG * SSD_HEAD_DIM

    def chunk_body(ci, carry):
        lane_sq = lax.broadcasted_iota(jnp.int32, (c, LANES), 1)
        causal = lax.broadcasted_iota(jnp.int32, (c, LANES), 0) >= lane_sq
        lane_lo = lane_sq < SSD_HEAD_DIM
        r0 = pl.multiple_of(ci * c, c)
        xbc = xbc_s[pl.ds(r0, c), :]
        xs = xbc[:, :SSD_INNER]
        bm = xbc[:, SSD_INNER:SSD_INNER + SSD_GROUPS * SSD_STATE].astype(BF16)
        cm = xbc[:, SSD_INNER + SSD_GROUPS * SSD_STATE:].astype(BF16)
        dt = dt_s[pl.ds(r0, c), :]
        da = dt * a_row
        a_cs = _dot(tril_ref[...], jnp.concatenate(_split_bf16(da, 3), axis=0))
        ecs = jnp.exp(a_cs)
        dte = jnp.exp(a_cs[c - 1:c, :] - a_cs)
        q = jnp.where(lane_sq < SSD_HEADS, dt, jnp.where(lane_sq < 2 * SSD_HEADS, ecs, dte))
        ex = _dot(jnp.concatenate(_split_bf16(q, 2), axis=1), e3_ref[...])
        dt_e = ex[:, :SSD_INNER]
        ecs_e = ex[:, SSD_INNER:2 * SSD_INNER]
        dte_e = ex[:, 2 * SSD_INNER:]
        colb = _dot(jnp.concatenate(_split_bf16(a_cs, 3), axis=1), f_ref[...])
        a_cs_t = a_cs.T
        xdt = xs * dt_e
        y_parts = []
        for g in range(SSD_GROUPS):
            cb = _dot_nt(cm[:, g * SSD_STATE:(g + 1) * SSD_STATE], bm[:, g * SSD_STATE:(g + 1) * SSD_STATE])
            for j in range(SSD_HPG // 2):
                h0 = g * SSD_HPG + 2 * j
                ms = []
                for h in (h0, h0 + 1):
                    seg = colb[:, h * LANES:(h + 1) * LANES] - a_cs_t[h:h + 1, :]
                    dec = jnp.exp(jnp.where(causal, seg, -jnp.inf))
                    ms.append((cb * dec).astype(BF16))
                xp = xdt[:, h0 * SSD_HEAD_DIM:(h0 + 2) * SSD_HEAD_DIM]
                rhs = jnp.concatenate([jnp.where(lane_lo, xp, 0.0), jnp.where(lane_lo, 0.0, xp)], axis=0)
                y_parts.append(_dot(jnp.concatenate(ms, axis=1), rhs.astype(BF16)))
        y = jnp.concatenate(y_parts, axis=1)
        y_off = []
        for g in range(SSD_GROUPS):
            hg = h_s[g]
            y_off.append(_dot(cm[:, g * SSD_STATE:(g + 1) * SSD_STATE], hg.astype(BF16)))
            xd = (xdt[:, g * gw:(g + 1) * gw] * dte_e[:, g * gw:(g + 1) * gw]).astype(BF16)
            new = _dot_tn(bm[:, g * SSD_STATE:(g + 1) * SSD_STATE], xd)
            h_s[g] = hg * ecs_e[c - 1:c, g * gw:(g + 1) * gw] + new
        y = y + jnp.concatenate(y_off, axis=1) * ecs_e + xs * dexp_ref[...]
        y = y * _silu(z_s[pl.ds(r0, c), :])
        yn = []
        for g in range(SSD_GROUPS):
            yg = y[:, g * gw:(g + 1) * gw]
            yn.append(yg * lax.rsqrt(jnp.mean(yg * yg, axis=-1, keepdims=True) + RMS_EPS))
        y = jnp.concatenate(yn, axis=1) * nw_ref[...]
        ycat_s[pl.ds(r0, c), 0:SSD_INNER] = y.astype(BF16)
        return carry

    lax.fori_loop(0, tm // c, chunk_body, 0)

    mix = _dot(ycat_s[...], w_out_ref[...])
    o_ref[...] = _layer_norm(ALPHA * x + mix, g_ref[...], b_ref[...])


def _attn_kernel(sink_ref, x_ref, w_qkv_ref, b_qkv_ref, w_o_ref, b_o_ref, g_ref, b_ref, o_ref,
                 q_s, k_s, v_s, a_s):
    tm = x_ref.shape[0]
    blk = ATTN_BLOCK
    first_tile = pl.program_id(1) == 0
    nq = ATTN_HEADS * ATTN_HEAD_DIM
    nkv2 = 2 * ATTN_KV_HEADS * ATTN_HEAD_DIM

    x = x_ref[...]
    qkv = _dot(x.astype(BF16), w_qkv_ref[...]) + b_qkv_ref[...]
    q_s[...] = (qkv[:, :nq] * (ATTN_HEAD_DIM ** -0.5)).astype(BF16)
    lane_kv = lax.broadcasted_iota(jnp.int32, (tm, nkv2), 1)
    lo = (lane_kv & (LANES - 1)) < ATTN_HEAD_DIM
    kd = qkv[:, nq:nq + nkv2]
    vd = qkv[:, nq + nkv2:]
    k_s[blk:, :nkv2] = jnp.where(lo, kd, 0.0).astype(BF16)
    k_s[blk:, nkv2:] = jnp.where(lo, 0.0, kd).astype(BF16)
    v_s[blk:, :nkv2] = jnp.where(lo, vd, 0.0).astype(BF16)
    v_s[blk:, nkv2:] = jnp.where(lo, 0.0, vd).astype(BF16)

    @pl.when(first_tile)
    def _():
        k_s[:blk, :] = jnp.zeros((blk, 2 * nkv2), BF16)
        v_s[:blk, :] = jnp.zeros((blk, 2 * nkv2), BF16)

    pairs = ATTN_HEADS // 2

    def block_body(n, carry):
        r0 = pl.multiple_of(n * blk, blk)
        qi = lax.broadcasted_iota(jnp.int32, (blk, 2 * blk), 0)
        kj = lax.broadcasted_iota(jnp.int32, (blk, 2 * blk), 1)
        lane_o = lax.broadcasted_iota(jnp.int32, (blk, LANES), 1) < ATTN_HEAD_DIM
        min_key = jnp.where(jnp.logical_and(first_tile, n == 0), blk, 0)
        valid = (kj > qi) & (kj <= qi + blk) & (kj >= min_key)
        for j in range(pairs):
            kvh = j // 2
            c0 = kvh * LANES
            qp = q_s[pl.ds(r0, blk), j * LANES:(j + 1) * LANES]
            kk = jnp.concatenate([k_s[pl.ds(r0, 2 * blk), c0:c0 + LANES],
                                  k_s[pl.ds(r0, 2 * blk), nkv2 + c0:nkv2 + c0 + LANES]], axis=0)
            vv = jnp.concatenate([v_s[pl.ds(r0, 2 * blk), c0:c0 + LANES],
                                  v_s[pl.ds(r0, 2 * blk), nkv2 + c0:nkv2 + c0 + LANES]], axis=0)
            s = _dot_nt(qp, kk)
            ps = []
            inv = []
            for t in range(2):
                sink = sink_ref[2 * j + t]
                sh = jnp.where(valid, s[:, t * 2 * blk:(t + 1) * 2 * blk], -jnp.inf)
                m = jnp.maximum(jnp.max(sh, axis=-1, keepdims=True), sink)
                p = jnp.exp(sh - m)
                den = jnp.sum(p, axis=-1, keepdims=True) + jnp.exp(sink - m)
                ps.append(p.astype(BF16))
                inv.append(1.0 / den)
            o = _dot(jnp.concatenate(ps, axis=1), vv)
            o = o * jnp.where(lane_o, inv[0], inv[1])
            a_s[pl.ds(r0, blk), j * LANES:(j + 1) * LANES] = o.astype(BF16)
        return carry

    lax.fori_loop(0, tm // blk, block_body, 0)

    k_s[:blk, :] = k_s[tm:, :]
    v_s[:blk, :] = v_s[tm:, :]

    mix = _dot(a_s[...], w_o_ref[...]) + b_o_ref[...]
    o_ref[...] = _layer_norm(ALPHA * x + mix, g_ref[...], b_ref[...])


def _ffn_kernel(x_ref, p_ref, wg_ref, wu_ref, wd_ref, g_ref, b_ref, wpg_ref, bpg_ref, wple_ref, o_ref):
    x = x_ref[...]
    xb = x.astype(BF16)
    h = (_silu(_dot(xb, wg_ref[...])) * _dot(xb, wu_ref[...])).astype(BF16)
    x2 = _layer_norm(ALPHA * x + _dot(h, wd_ref[...]), g_ref[...], b_ref[...])
    gate = _sigmoid(_dot(x2.astype(BF16), wpg_ref[...]) + bpg_ref[...])
    pe = _dot(p_ref[...].astype(BF16), wple_ref[...])
    o_ref[...] = x2 + pe * gate


def _resident(shape):
    return pl.BlockSpec(shape, lambda *_: (0,) * len(shape), pipeline_mode=pl.Buffered(1))


def _row(v):
    return v.reshape(1, -1).astype(F32)


def _pad_lanes(v, width):
    return jnp.pad(v, ((0, 0), (0, width - v.shape[1])))


def _mixer0(x2d, batch, w_in, conv_w, conv_b, dt_bias, a_log, d_skip, norm_w, sc_conv_w, w_out, ln_g, ln_b):
    t = x2d.shape[0]
    n_l = t // batch // TM_MIX0
    o1 = SSD_INNER
    o2 = o1 + SSD_XBC
    o3 = o2 + SSD_HEADS
    w_in_p = jnp.concatenate(
        [w_in[:, :o2], w_in[:, o3:], _pad_lanes(jnp.tile(w_in[:, o2:o3], (1, DT_REP)), LANES)], axis=1).astype(BF16)
    dtb = _pad_lanes(jnp.tile(_row(dt_bias), (1, DT_REP)), LANES)
    alog = _pad_lanes(jnp.tile(_row(a_log), (1, DT_REP)), LANES)
    dexp = jnp.repeat(_row(d_skip), SSD_HEAD_DIM, axis=1)
    r = jnp.arange(LANES)[:, None]
    col = jnp.arange(DT_REP * SSD_INNER)[None, :]
    e1 = ((r < DT_REP * SSD_HEADS) & (col // SSD_HEAD_DIM == r)).astype(BF16)
    e3 = jnp.concatenate([e1, e1], axis=0)
    colf = jnp.arange(SSD_HEADS * LANES)[None, :]
    f1 = (colf // LANES == r).astype(BF16)
    f3 = jnp.concatenate([f1, f1, f1], axis=0)
    tri = (jnp.arange(SSD_CHUNK)[:, None] >= jnp.arange(SSD_CHUNK)[None, :]).astype(BF16)
    tril3 = jnp.concatenate([tri, tri, tri], axis=1)

    tok = lambda width: pl.BlockSpec((TM_MIX0, width), lambda b, l: (b * n_l + l, 0))
    return pl.pallas_call(
        _mixer0_kernel,
        grid=(batch, n_l),
        in_specs=[tok(D_MODEL), _resident((D_MODEL, MIX_IN_P)), _resident((SSD_CONV, SSD_XBC)),
                  _resident((1, SSD_XBC)), _resident((1, LANES)), _resident((1, LANES)),
                  _resident((1, SSD_INNER)), _resident((1, SSD_INNER)), _resident((SC_CONV, SC_DIM)),
                  _resident((MIX_OUT, D_MODEL)), _resident((1, D_MODEL)), _resident((1, D_MODEL)),
                  _resident(e3.shape), _resident(f3.shape), _resident(tril3.shape)],
        out_specs=tok(D_MODEL),
        out_shape=jax.ShapeDtypeStruct((t, D_MODEL), F32),
        scratch_shapes=[pltpu.VMEM((TM_MIX0, SSD_INNER), F32), pltpu.VMEM((TM_MIX0, SSD_XBC), F32),
                        pltpu.VMEM((TM_MIX0, LANES), F32), pltpu.VMEM((TM_MIX0, MIX_OUT), BF16),
                        pltpu.VMEM((SUBLANES, SSD_XBC), F32), pltpu.VMEM((SUBLANES, SC_DIM), F32),
                        pltpu.VMEM((SSD_GROUPS, SSD_STATE, SSD_HPG * SSD_HEAD_DIM), F32)],
        compiler_params=pltpu.CompilerParams(dimension_semantics=("arbitrary", "arbitrary"),
                                             vmem_limit_bytes=VMEM_LIMIT_BYTES),
        name="mixer0_ssd_shortconv",
    )(x2d, w_in_p, conv_w.astype(F32), _row(conv_b), dtb, alog, dexp, _row(norm_w), sc_conv_w.astype(F32),
      w_out.astype(BF16), _row(ln_g), _row(ln_b), e3, f3, tril3)


def _attn(x2d, batch, w_qkv, b_qkv, sinks, w_o, b_o, ln_g, ln_b):
    t = x2d.shape[0]
    n_l = t // batch // TM_ATTN
    nq = ATTN_HEADS * ATTN_HEAD_DIM
    nkv = ATTN_KV_HEADS * ATTN_HEAD_DIM

    def dup_heads(m):
        lead = m.shape[:-1]
        m = m.reshape(lead + (ATTN_KV_HEADS, 1, ATTN_HEAD_DIM))
        return jnp.broadcast_to(m, lead + (ATTN_KV_HEADS, 2, ATTN_HEAD_DIM)).reshape(lead + (2 * nkv,))

    w_p = jnp.concatenate([w_qkv[:, :nq], dup_heads(w_qkv[:, nq:nq + nkv]), dup_heads(w_qkv[:, nq + nkv:])],
                          axis=1).astype(BF16)
    b_p = _row(jnp.concatenate([b_qkv[:nq], dup_heads(b_qkv[nq:nq + nkv]), dup_heads(b_qkv[nq + nkv:])]))
    width = nq + 4 * nkv
    tok = pl.BlockSpec((TM_ATTN, D_MODEL), lambda b, l: (b * n_l + l, 0))
    return pl.pallas_call(
        _attn_kernel,
        grid=(batch, n_l),
        in_specs=[pl.BlockSpec(memory_space=pltpu.SMEM), tok, _resident((D_MODEL, width)), _resident((1, width)),
                  _resident((nq, D_MODEL)), _resident((1, D_MODEL)), _resident((1, D_MODEL)),
                  _resident((1, D_MODEL))],
        out_specs=tok,
        out_shape=jax.ShapeDtypeStruct((t, D_MODEL), F32),
        scratch_shapes=[pltpu.VMEM((TM_ATTN, nq), BF16), pltpu.VMEM((TM_ATTN + ATTN_BLOCK, 4 * nkv), BF16),
                        pltpu.VMEM((TM_ATTN + ATTN_BLOCK, 4 * nkv), BF16), pltpu.VMEM((TM_ATTN, nq), BF16)],
        compiler_params=pltpu.CompilerParams(dimension_semantics=("arbitrary", "arbitrary"),
                                             vmem_limit_bytes=VMEM_LIMIT_BYTES),
        name="mixer1_swa",
    )(sinks.astype(F32), x2d, w_p, b_p, w_o.astype(BF16), _row(b_o), _row(ln_g), _row(ln_b))


def _ffn(x2d, p2d, w_gate, w_up, w_down, ln_g, ln_b, w_ple, w_ple_gate, b_ple_gate):
    t = x2d.shape[0]
    tok = lambda width: pl.BlockSpec((TM_FFN, width), lambda i: (i, 0))
    return pl.pallas_call(
        _ffn_kernel,
        grid=(t // TM_FFN,),
        in_specs=[tok(D_MODEL), tok(PLE_DIM), _resident((D_MODEL, D_FF)), _resident((D_MODEL, D_FF)),
                  _resident((D_FF, D_MODEL)), _resident((1, D_MODEL)), _resident((1, D_MODEL)),
                  _resident((D_MODEL, D_MODEL)), _resident((1, D_MODEL)), _resident((PLE_DIM, D_MODEL))],
        out_specs=tok(D_MODEL),
        out_shape=jax.ShapeDtypeStruct((t, D_MODEL), F32),
        compiler_params=pltpu.CompilerParams(dimension_semantics=("arbitrary",),
                                             vmem_limit_bytes=VMEM_LIMIT_BYTES),
        name="ffn_ple",
    )(x2d, p2d, w_gate.astype(BF16), w_up.astype(BF16), w_down.astype(BF16), _row(ln_g), _row(ln_b),
      w_ple_gate.astype(BF16), _row(b_ple_gate), w_ple.astype(BF16))


def kernel(x, p, w_in_mix, ssd_conv_w, ssd_conv_b, ssd_dt_bias, ssd_a_log, ssd_d, ssd_norm_w, sc_conv_w, w_out_mix, w_qkv, b_qkv, attn_sinks, w_o, b_o, ln_mix_g, ln_mix_b, w_ffn_gate, w_ffn_up, w_ffn_down, ln_ffn_g, ln_ffn_b, w_ple, w_ple_gate, b_ple_gate):
    batch, seq, d = x.shape
    assert d == D_MODEL and seq % TM_MIX0 == 0 and seq % TM_ATTN == 0 and (batch * seq) % TM_FFN == 0
    assert p.shape == (DEPTH, batch, seq, PLE_DIM)
    h = x.reshape(batch * seq, d)
    p2d = p.reshape(DEPTH, batch * seq, PLE_DIM)

    def ffn(h, i):
        return _ffn(h, p2d[i], w_ffn_gate[i], w_ffn_up[i], w_ffn_down[i], ln_ffn_g[i], ln_ffn_b[i],
                    w_ple[i], w_ple_gate[i], b_ple_gate[i])

    h = _mixer0(h, batch, w_in_mix[0], ssd_conv_w[0], ssd_conv_b[0], ssd_dt_bias[0], ssd_a_log[0], ssd_d[0],
                ssd_norm_w[0], sc_conv_w[0], w_out_mix[0], ln_mix_g[0], ln_mix_b[0])
    h = ffn(h, 0)
    h = _attn(h, batch, w_qkv[0], b_qkv[0], attn_sinks[0], w_o[0], b_o[0], ln_mix_g[1], ln_mix_b[1])
    h = ffn(h, 1)
    return h.reshape(batch, seq, d)
```

```python
import functools

import jax
import jax.numpy as jnp
from jax import lax
from jax.experimental import pallas as pl
from jax.experimental.pallas import tpu as pltpu

F32 = jnp.float32
BF16 = jnp.bfloat16

D_MODEL = 1024
DEPTH = 2
PLE_DIM = 256
SSD_HEADS = 16
SSD_HEAD_DIM = 64
SSD_INNER = SSD_HEADS * SSD_HEAD_DIM
SSD_GROUPS = 2
SSD_HPG = SSD_HEADS // SSD_GROUPS
SSD_STATE = 128
SSD_CONV = 4
SSD_CHUNK = 128
SSD_XBC = SSD_INNER + 2 * SSD_GROUPS * SSD_STATE
SC_DIM = 1024
SC_CONV = 3
MIX_OUT = SSD_INNER + SC_DIM
ATTN_HEADS = 16
ATTN_KV_HEADS = 4
ATTN_HEAD_DIM = 64
ATTN_BLOCK = 128
D_FF = ((8 * D_MODEL + 3 * 256 - 1) // (3 * 256)) * 256
ALPHA = (2 * DEPTH) ** 0.25
LN_EPS = 1e-5
RMS_EPS = 1e-5

LANES = 128
SUBLANES = 8
V7X_VMEM_BYTES = 64 * 1024 * 1024
VMEM_LIMIT_BYTES = V7X_VMEM_BYTES - 8 * 1024 * 1024

TM_MIX0 = 512
TM_ATTN = 512
TM_FFN = 512

_C_Z = 0
_C_XBC = _C_Z + SSD_INNER
_C_SCB = _C_XBC + SSD_XBC
_C_SCC = _C_SCB + SC_DIM
_C_SCH = _C_SCC + SC_DIM
_C_DT = _C_SCH + SC_DIM
MIX_IN_P = _C_DT + LANES
DT_REP = 3


def _dot(a, b):
    return jnp.dot(a, b, preferred_element_type=F32)


def _dot_nt(a, b):
    return lax.dot_general(a, b, (((1,), (1,)), ((), ())), preferred_element_type=F32)


def _dot_tn(a, b):
    return lax.dot_general(a, b, (((0,), (0,)), ((), ())), preferred_element_type=F32)


def _sigmoid(x):
    return 1.0 / (1.0 + jnp.exp(-x))


def _silu(x):
    return x * _sigmoid(x)


def _layer_norm(r, g, b):
    mu = jnp.mean(r, axis=-1, keepdims=True)
    xc = r - mu
    var = jnp.mean(xc * xc, axis=-1, keepdims=True)
    return xc * lax.rsqrt(var + LN_EPS) * g + b


def _split_bf16(v, parts):
    out = []
    rem = v
    for i in range(parts):
        p = rem.astype(BF16)
        out.append(p)
        if i + 1 < parts:
            rem = rem - p.astype(F32)
    return out


def _causal_conv(cur, prev_tail, w_ref, width):
    cols = cur.shape[1]
    rows = lax.broadcasted_iota(jnp.int32, (SUBLANES, cols), 0)
    acc = cur * w_ref[width - 1:width, :]
    for s in range(1, width):
        rolled = pltpu.roll(cur, s, 0)
        head = jnp.where(rows < s, pltpu.roll(prev_tail, s, 0), rolled[0:SUBLANES])
        shifted = jnp.concatenate([head, rolled[SUBLANES:]], axis=0)
        acc = acc + shifted * w_ref[width - 1 - s:width - s, :]
    return acc


def _mixer0_kernel(x_ref, w_in_ref, cw_ref, cb_ref, dtb_ref, alog_ref, dexp_ref, nw_ref, scw_ref, w_out_ref,
                   g_ref, b_ref, e3_ref, f_ref, tril_ref, o_ref,
                   z_s, xbc_s, dt_s, ycat_s, prev_xbc_s, prev_sc_s, h_s):
    tm = x_ref.shape[0]

    @pl.when(pl.program_id(1) == 0)
    def _():
        prev_xbc_s[...] = jnp.zeros_like(prev_xbc_s)
        prev_sc_s[...] = jnp.zeros_like(prev_sc_s)
        h_s[...] = jnp.zeros_like(h_s)

    x = x_ref[...]
    xb = x.astype(BF16)

    def proj(lo, hi):
        return _dot(xb, w_in_ref[:, lo:hi])

    z_s[...] = proj(_C_Z, _C_XBC)

    xbc_raw = proj(_C_XBC, _C_SCB)
    xbc_s[...] = _silu(_causal_conv(xbc_raw, prev_xbc_s[...], cw_ref, SSD_CONV) + cb_ref[...])
    prev_xbc_s[...] = xbc_raw[tm - SUBLANES:tm]

    lane_row = lax.broadcasted_iota(jnp.int32, (1, LANES), 1)
    dt_pre = proj(_C_DT, MIX_IN_P) + dtb_ref[...]
    dt_s[...] = jnp.maximum(dt_pre, 0.0) + jnp.log1p(jnp.exp(-jnp.abs(dt_pre)))
    a_row = jnp.where(lane_row < DT_REP * SSD_HEADS, -jnp.exp(alog_ref[...]), 0.0)

    u = proj(_C_SCC, _C_SCH) * proj(_C_SCH, _C_DT)
    y_sc = proj(_C_SCB, _C_SCC) * _causal_conv(u, prev_sc_s[...], scw_ref, SC_CONV)
    ycat_s[:, SSD_INNER:MIX_OUT] = y_sc.astype(BF16)
    prev_sc_s[...] = u[tm - SUBLANES:tm]

    c = SSD_CHUNK
    gw = SSD_HPG * SSD_HEAD_DIM

    def chunk_body(ci, carry):
        lane_sq = lax.broadcasted_iota(jnp.int32, (c, LANES), 1)
        causal = lax.broadcasted_iota(jnp.int32, (c, LANES), 0) >= lane_sq
        lane_lo = lane_sq < SSD_HEAD_DIM
        r0 = pl.multiple_of(ci * c, c)
        xbc = xbc_s[pl.ds(r0, c), :]
        xs = xbc[:, :SSD_INNER]
        bm = xbc[:, SSD_INNER:SSD_INNER + SSD_GROUPS * SSD_STATE].astype(BF16)
        cm = xbc[:, SSD_INNER + SSD_GROUPS * SSD_STATE:].astype(BF16)
        dt = dt_s[pl.ds(r0, c), :]
        da = dt * a_row
        a_cs = _dot(tril_ref[...], jnp.concatenate(_split_bf16(da, 3), axis=0))
        ecs = jnp.exp(a_cs)
        dte = jnp.exp(a_cs[c - 1:c, :] - a_cs)
        q = jnp.where(lane_sq < SSD_HEADS, dt, jnp.where(lane_sq < 2 * SSD_HEADS, ecs, dte))
        ex = _dot(jnp.concatenate(_split_bf16(q, 2), axis=1), e3_ref[...])
        dt_e = ex[:, :SSD_INNER]
        ecs_e = ex[:, SSD_INNER:2 * SSD_INNER]
        dte_e = ex[:, 2 * SSD_INNER:]
        colb = _dot(jnp.concatenate(_split_bf16(a_cs, 3), axis=1), f_ref[...])
        a_cs_t = a_cs.T
        xdt = xs * dt_e
        y_parts = []
        for g in range(SSD_GROUPS):
            cb = _dot_nt(cm[:, g * SSD_STATE:(g + 1) * SSD_STATE], bm[:, g * SSD_STATE:(g + 1) * SSD_STATE])
            for j in range(SSD_HPG // 2):
                h0 = g * SSD_HPG + 2 * j
                ms = []
                for h in (h0, h0 + 1):
                    seg = colb[:, h * LANES:(h + 1) * LANES] - a_cs_t[h:h + 1, :]
                    dec = jnp.exp(jnp.where(causal, seg, -jnp.inf))
                    ms.append((cb * dec).astype(BF16))
                xp = xdt[:, h0 * SSD_HEAD_DIM:(h0 + 2) * SSD_HEAD_DIM]
                rhs = jnp.concatenate([jnp.where(lane_lo, xp, 0.0), jnp.where(lane_lo, 0.0, xp)], axis=0)
                y_parts.append(_dot(jnp.concatenate(ms, axis=1), rhs.astype(BF16)))
        y = jnp.concatenate(y_parts, axis=1)
        y_off = []
        for g in range(SSD_GROUPS):
            hg = h_s[g]
            y_off.append(_dot(cm[:, g * SSD_STATE:(g + 1) * SSD_STATE], hg.astype(BF16)))
            xd = (xdt[:, g * gw:(g + 1) * gw] * dte_e[:, g * gw:(g + 1) * gw]).astype(BF16)
            new = _dot_tn(bm[:, g * SSD_STATE:(g + 1) * SSD_STATE], xd)
            h_s[g] = hg * ecs_e[c - 1:c, g * gw:(g + 1) * gw] + new
        y = y + jnp.concatenate(y_off, axis=1) * ecs_e + xs * dexp_ref[...]
        y = y * _silu(z_s[pl.ds(r0, c), :])
        yn = []
        for g in range(SSD_GROUPS):
            yg = y[:, g * gw:(g + 1) * gw]
            yn.append(yg * lax.rsqrt(jnp.mean(yg * yg, axis=-1, keepdims=True) + RMS_EPS))
        y = jnp.concatenate(yn, axis=1) * nw_ref[...]
        ycat_s[pl.ds(r0, c), 0:SSD_INNER] = y.astype(BF16)
        return carry

    lax.fori_loop(0, tm // c, chunk_body, 0)

    mix = _dot(ycat_s[...], w_out_ref[...])
    o_ref[...] = _layer_norm(ALPHA * x + mix, g_ref[...], b_ref[...])


def _attn_kernel(sink_ref, x_ref, w_qkv_ref, b_qkv_ref, w_o_ref, b_o_ref, g_ref, b_ref, ones_ref, o_ref,
                 q_s, kt_s, v_s, a_s, kprev_s, vprev_s):
    tm = x_ref.shape[0]
    blk = ATTN_BLOCK
    first_tile = pl.program_id(1) == 0
    nq = ATTN_HEADS * ATTN_HEAD_DIM
    nkv2 = 2 * ATTN_KV_HEADS * ATTN_HEAD_DIM
    half = ATTN_HEAD_DIM

    x = x_ref[...]
    qkv = _dot(x.astype(BF16), w_qkv_ref[...]) + b_qkv_ref[...]
    q_s[...] = (qkv[:, :nq] * (ATTN_HEAD_DIM ** -0.5)).astype(BF16)
    @pl.when(first_tile)
    def _():
        kprev_s[...] = jnp.zeros_like(kprev_s)
        vprev_s[...] = jnp.zeros_like(vprev_s)

    kt_s[:, :blk] = kprev_s[...]
    kt = qkv[:, nq:nq + nkv2].T
    lo_r = (lax.broadcasted_iota(jnp.int32, (nkv2, tm), 0) & (LANES - 1)) < half
    for r0, part in ((0, jnp.where(lo_r, kt, 0.0).astype(BF16)), (nkv2, jnp.where(lo_r, 0.0, kt).astype(BF16))):
        kt_s[r0:r0 + nkv2, blk:] = part
        kprev_s[r0:r0 + nkv2, :] = part[:, tm - blk:]
    v_s[:blk, :] = vprev_s[...]
    vd = qkv[:, nq + nkv2:]
    lo = (lax.broadcasted_iota(jnp.int32, (tm, nkv2), 1) & (LANES - 1)) < half
    for c0, part in ((0, jnp.where(lo, vd, 0.0).astype(BF16)), (nkv2, jnp.where(lo, 0.0, vd).astype(BF16))):
        v_s[blk:, c0:c0 + nkv2] = part
        vprev_s[:, c0:c0 + nkv2] = part[tm - blk:]

    qi = lax.broadcasted_iota(jnp.int32, (blk, 2 * blk), 0)
    kj = lax.broadcasted_iota(jnp.int32, (blk, 2 * blk), 1)
    band = (kj > qi) & (kj <= qi + blk)
    band0 = band & (kj >= jnp.where(first_tile, blk, 0))
    lane_o = lax.broadcasted_iota(jnp.int32, (blk, LANES), 1) < half

    def scores(n, j):
        c0 = (j // 2) * LANES
        keys = slice(n * blk, (n + 2) * blk)
        kk = jnp.concatenate([kt_s[c0:c0 + LANES, keys], kt_s[nkv2 + c0:nkv2 + c0 + LANES, keys]], axis=1)
        return _dot(q_s[n * blk:(n + 1) * blk, j * LANES:(j + 1) * LANES], kk)

    def probs(n, j, s):
        valid = band0 if n == 0 else band
        ps = []
        es = []
        for t in range(2):
            sink = sink_ref[2 * j + t]
            sh = jnp.where(valid, s[:, t * 2 * blk:(t + 1) * 2 * blk], -jnp.inf)
            m = jnp.maximum(jnp.max(sh, axis=-1, keepdims=True), sink)
            ps.append(jnp.exp(sh - m).astype(BF16))
            es.append(jnp.exp(sink - m))
        return jnp.concatenate(ps, axis=1), jnp.where(lane_o, es[0], es[1])

    def output(n, j, p, es):
        c0 = (j // 2) * LANES
        keys = slice(n * blk, (n + 2) * blk)
        vv = jnp.concatenate([v_s[keys, c0:c0 + LANES], v_s[keys, nkv2 + c0:nkv2 + c0 + LANES]], axis=0)
        r = _dot(p, jnp.concatenate([vv, ones_ref[...]], axis=1))
        a_s[n * blk:(n + 1) * blk, j * LANES:(j + 1) * LANES] = (r[:, :LANES] / (r[:, LANES:] + es)).astype(BF16)

    steps = [(n, j) for n in range(tm // blk) for j in range(ATTN_HEADS // 2)]
    s_val = {}
    p_val = {}
    for i in range(len(steps) + 2):
        if i < len(steps):
            s_val[i] = scores(*steps[i])
        if 0 <= i - 1 < len(steps):
            p_val[i - 1] = probs(*steps[i - 1], s_val.pop(i - 1))
        if 0 <= i - 2 < len(steps):
            output(*steps[i - 2], *p_val.pop(i - 2))

    mix = _dot(a_s[...], w_o_ref[...]) + b_o_ref[...]
    o_ref[...] = _layer_norm(ALPHA * x + mix, g_ref[...], b_ref[...])


def _ffn_kernel(x_ref, p_ref, wg_ref, wu_ref, wd_ref, g_ref, b_ref, wpg_ref, bpg_ref, wple_ref, o_ref):
    x = x_ref[...]
    xb = x.astype(BF16)
    h = (_silu(_dot(xb, wg_ref[...])) * _dot(xb, wu_ref[...])).astype(BF16)
    x2 = _layer_norm(ALPHA * x + _dot(h, wd_ref[...]), g_ref[...], b_ref[...])
    gate = _sigmoid(_dot(x2.astype(BF16), wpg_ref[...]) + bpg_ref[...])
    pe = _dot(p_ref[...].astype(BF16), wple_ref[...])
    o_ref[...] = x2 + pe * gate


def _resident(shape):
    return pl.BlockSpec(shape, lambda *_: (0,) * len(shape), pipeline_mode=pl.Buffered(1))


def _row(v):
    return v.reshape(1, -1).astype(F32)


def _pad_lanes(v, width):
    return jnp.pad(v, ((0, 0), (0, width - v.shape[1])))


def _mixer0(x2d, batch, w_in, conv_w, conv_b, dt_bias, a_log, d_skip, norm_w, sc_conv_w, w_out, ln_g, ln_b):
    t = x2d.shape[0]
    n_l = t // batch // TM_MIX0
    o1 = SSD_INNER
    o2 = o1 + SSD_XBC
    o3 = o2 + SSD_HEADS
    w_in_p = jnp.concatenate(
        [w_in[:, :o2], w_in[:, o3:], _pad_lanes(jnp.tile(w_in[:, o2:o3], (1, DT_REP)), LANES)], axis=1).astype(BF16)
    dtb = _pad_lanes(jnp.tile(_row(dt_bias), (1, DT_REP)), LANES)
    alog = _pad_lanes(jnp.tile(_row(a_log), (1, DT_REP)), LANES)
    dexp = jnp.repeat(_row(d_skip), SSD_HEAD_DIM, axis=1)
    r = jnp.arange(LANES)[:, None]
    col = jnp.arange(DT_REP * SSD_INNER)[None, :]
    e1 = ((r < DT_REP * SSD_HEADS) & (col // SSD_HEAD_DIM == r)).astype(BF16)
    e3 = jnp.concatenate([e1, e1], axis=0)
    colf = jnp.arange(SSD_HEADS * LANES)[None, :]
    f1 = (colf // LANES == r).astype(BF16)
    f3 = jnp.concatenate([f1, f1, f1], axis=0)
    tri = (jnp.arange(SSD_CHUNK)[:, None] >= jnp.arange(SSD_CHUNK)[None, :]).astype(BF16)
    tril3 = jnp.concatenate([tri, tri, tri], axis=1)

    tok = lambda width: pl.BlockSpec((TM_MIX0, width), lambda b, l: (b * n_l + l, 0))
    return pl.pallas_call(
        _mixer0_kernel,
        grid=(batch, n_l),
        in_specs=[tok(D_MODEL), _resident((D_MODEL, MIX_IN_P)), _resident((SSD_CONV, SSD_XBC)),
                  _resident((1, SSD_XBC)), _resident((1, LANES)), _resident((1, LANES)),
                  _resident((1, SSD_INNER)), _resident((1, SSD_INNER)), _resident((SC_CONV, SC_DIM)),
                  _resident((MIX_OUT, D_MODEL)), _resident((1, D_MODEL)), _resident((1, D_MODEL)),
                  _resident(e3.shape), _resident(f3.shape), _resident(tril3.shape)],
        out_specs=tok(D_MODEL),
        out_shape=jax.ShapeDtypeStruct((t, D_MODEL), F32),
        scratch_shapes=[pltpu.VMEM((TM_MIX0, SSD_INNER), F32), pltpu.VMEM((TM_MIX0, SSD_XBC), F32),
                        pltpu.VMEM((TM_MIX0, LANES), F32), pltpu.VMEM((TM_MIX0, MIX_OUT), BF16),
                        pltpu.VMEM((SUBLANES, SSD_XBC), F32), pltpu.VMEM((SUBLANES, SC_DIM), F32),
                        pltpu.VMEM((SSD_GROUPS, SSD_STATE, SSD_HPG * SSD_HEAD_DIM), F32)],
        compiler_params=pltpu.CompilerParams(dimension_semantics=("arbitrary", "arbitrary"),
                                             vmem_limit_bytes=VMEM_LIMIT_BYTES),
        name="mixer0_ssd_shortconv",
    )(x2d, w_in_p, conv_w.astype(F32), _row(conv_b), dtb, alog, dexp, _row(norm_w), sc_conv_w.astype(F32),
      w_out.astype(BF16), _row(ln_g), _row(ln_b), e3, f3, tril3)


def _attn(x2d, batch, w_qkv, b_qkv, sinks, w_o, b_o, ln_g, ln_b):
    t = x2d.shape[0]
    n_l = t // batch // TM_ATTN
    nq = ATTN_HEADS * ATTN_HEAD_DIM
    nkv = ATTN_KV_HEADS * ATTN_HEAD_DIM

    def dup_heads(m):
        lead = m.shape[:-1]
        m = m.reshape(lead + (ATTN_KV_HEADS, 1, ATTN_HEAD_DIM))
        return jnp.broadcast_to(m, lead + (ATTN_KV_HEADS, 2, ATTN_HEAD_DIM)).reshape(lead + (2 * nkv,))

    w_p = jnp.concatenate([w_qkv[:, :nq], dup_heads(w_qkv[:, nq:nq + nkv]), dup_heads(w_qkv[:, nq + nkv:])],
                          axis=1).astype(BF16)
    b_p = _row(jnp.concatenate([b_qkv[:nq], dup_heads(b_qkv[nq:nq + nkv]), dup_heads(b_qkv[nq + nkv:])]))
    width = nq + 4 * nkv
    ones_blk = (jnp.arange(4 * ATTN_BLOCK)[:, None] // (2 * ATTN_BLOCK) == jnp.arange(LANES)[None, :] // ATTN_HEAD_DIM
                ).astype(BF16)
    tok = pl.BlockSpec((TM_ATTN, D_MODEL), lambda b, l: (b * n_l + l, 0))
    return pl.pallas_call(
        _attn_kernel,
        grid=(batch, n_l),
        in_specs=[pl.BlockSpec(memory_space=pltpu.SMEM), tok, _resident((D_MODEL, width)), _resident((1, width)),
                  _resident((nq, D_MODEL)), _resident((1, D_MODEL)), _resident((1, D_MODEL)),
                  _resident((1, D_MODEL)), _resident(ones_blk.shape)],
        out_specs=tok,
        out_shape=jax.ShapeDtypeStruct((t, D_MODEL), F32),
        scratch_shapes=[pltpu.VMEM((TM_ATTN, nq), BF16), pltpu.VMEM((4 * nkv, TM_ATTN + ATTN_BLOCK), BF16),
                        pltpu.VMEM((TM_ATTN + ATTN_BLOCK, 4 * nkv), BF16), pltpu.VMEM((TM_ATTN, nq), BF16),
                        pltpu.VMEM((4 * nkv, ATTN_BLOCK), BF16), pltpu.VMEM((ATTN_BLOCK, 4 * nkv), BF16)],
        compiler_params=pltpu.CompilerParams(dimension_semantics=("arbitrary", "arbitrary"),
                                             vmem_limit_bytes=VMEM_LIMIT_BYTES),
        name="mixer1_swa",
    )(sinks.astype(F32), x2d, w_p, b_p, w_o.astype(BF16), _row(b_o), _row(ln_g), _row(ln_b), ones_blk)


def _ffn(x2d, layer, p3d, w_gate, w_up, w_down, ln_g, ln_b, w_ple, w_ple_gate, b_ple_gate):
    t = x2d.shape[0]
    tok = lambda width: pl.BlockSpec((TM_FFN, width), lambda i: (i, 0))

    def layer_resident(rows, cols):
        return pl.BlockSpec((None, rows, cols), lambda i: (layer, 0, 0), pipeline_mode=pl.Buffered(1))

    return pl.pallas_call(
        _ffn_kernel,
        grid=(t // TM_FFN,),
        in_specs=[tok(D_MODEL), pl.BlockSpec((None, TM_FFN, PLE_DIM), lambda i: (layer, i, 0)),
                  layer_resident(D_MODEL, D_FF), layer_resident(D_MODEL, D_FF), layer_resident(D_FF, D_MODEL),
                  _resident((1, D_MODEL)), _resident((1, D_MODEL)),
                  layer_resident(D_MODEL, D_MODEL), _resident((1, D_MODEL)), layer_resident(PLE_DIM, D_MODEL)],
        out_specs=tok(D_MODEL),
        out_shape=jax.ShapeDtypeStruct((t, D_MODEL), F32),
        compiler_params=pltpu.CompilerParams(dimension_semantics=("arbitrary",),
                                             vmem_limit_bytes=VMEM_LIMIT_BYTES),
        name="ffn_ple",
    )(x2d, p3d, w_gate, w_up, w_down, _row(ln_g[layer]), _row(ln_b[layer]),
      w_ple_gate, _row(b_ple_gate[layer]), w_ple)


def kernel(x, p, w_in_mix, ssd_conv_w, ssd_conv_b, ssd_dt_bias, ssd_a_log, ssd_d, ssd_norm_w, sc_conv_w, w_out_mix, w_qkv, b_qkv, attn_sinks, w_o, b_o, ln_mix_g, ln_mix_b, w_ffn_gate, w_ffn_up, w_ffn_down, ln_ffn_g, ln_ffn_b, w_ple, w_ple_gate, b_ple_gate):
    batch, seq, d = x.shape
    assert d == D_MODEL and seq % TM_MIX0 == 0 and seq % TM_ATTN == 0 and (batch * seq) % TM_FFN == 0
    assert p.shape == (DEPTH, batch, seq, PLE_DIM)
    h = x.reshape(batch * seq, d)
    p2d = p.reshape(DEPTH, batch * seq, PLE_DIM)

    ffn_w = [w.astype(BF16) for w in (w_ffn_gate, w_ffn_up, w_ffn_down)]
    ple_w = w_ple.astype(BF16)
    ple_gate_w = w_ple_gate.astype(BF16)

    def ffn(h, i):
        return _ffn(h, i, p2d, *ffn_w, ln_ffn_g, ln_ffn_b, ple_w, ple_gate_w, b_ple_gate)

    h = _mixer0(h, batch, w_in_mix[0], ssd_conv_w[0], ssd_conv_b[0], ssd_dt_bias[0], ssd_a_log[0], ssd_d[0],
                ssd_norm_w[0], sc_conv_w[0], w_out_mix[0], ln_mix_g[0], ln_mix_b[0])
    h = ffn(h, 0)
    h = _attn(h, batch, w_qkv[0], b_qkv[0], attn_sinks[0], w_o[0], b_o[0], ln_mix_g[1], ln_mix_b[1])
    h = ffn(h, 1)
    return h.reshape(batch, seq, d)
```

```python
import functools

import jax
import jax.numpy as jnp
from jax import lax
from jax.experimental import pallas as pl
from jax.experimental.pallas import tpu as pltpu

F32 = jnp.float32
BF16 = jnp.bfloat16

D_MODEL = 1024
DEPTH = 2
PLE_DIM = 256
SSD_HEADS = 16
SSD_HEAD_DIM = 64
SSD_INNER = SSD_HEADS * SSD_HEAD_DIM
SSD_GROUPS = 2
SSD_HPG = SSD_HEADS // SSD_GROUPS
SSD_STATE = 128
SSD_CONV = 4
SSD_CHUNK = 128
SSD_XBC = SSD_INNER + 2 * SSD_GROUPS * SSD_STATE
SC_DIM = 1024
SC_CONV = 3
MIX_OUT = SSD_INNER + SC_DIM
ATTN_HEADS = 16
ATTN_KV_HEADS = 4
ATTN_HEAD_DIM = 64
ATTN_BLOCK = 128
D_FF = ((8 * D_MODEL + 3 * 256 - 1) // (3 * 256)) * 256
ALPHA = (2 * DEPTH) ** 0.25
LN_EPS = 1e-5
RMS_EPS = 1e-5

LANES = 128
SUBLANES = 8
V7X_VMEM_BYTES = 64 * 1024 * 1024
VMEM_LIMIT_BYTES = V7X_VMEM_BYTES - 8 * 1024 * 1024

TM_MIX0 = 512
TM_ATTN = 512
TM_FFN = 512

_C_Z = 0
_C_XBC = _C_Z + SSD_INNER
_C_SCB = _C_XBC + SSD_XBC
_C_SCC = _C_SCB + SC_DIM
_C_SCH = _C_SCC + SC_DIM
_C_DT = _C_SCH + SC_DIM
MIX_IN_P = _C_DT + LANES
DT_REP = 2
COL_CHUNK = 256


def _dot(a, b):
    return jnp.dot(a, b, preferred_element_type=F32)


def _dot_nt(a, b):
    return lax.dot_general(a, b, (((1,), (1,)), ((), ())), preferred_element_type=F32)


def _dot_tn(a, b):
    return lax.dot_general(a, b, (((0,), (0,)), ((), ())), preferred_element_type=F32)


def _sigmoid(x):
    return 1.0 / (1.0 + jnp.exp(-x))


def _silu(x):
    return x * _sigmoid(x)


def _layer_norm(r, g, b):
    mu = jnp.mean(r, axis=-1, keepdims=True)
    xc = r - mu
    var = jnp.mean(xc * xc, axis=-1, keepdims=True)
    return xc * lax.rsqrt(var + LN_EPS) * g + b


def _split_bf16(v, parts):
    out = []
    rem = v
    for i in range(parts):
        p = rem.astype(BF16)
        out.append(p)
        if i + 1 < parts:
            rem = rem - p.astype(F32)
    return out


def _causal_conv(cur, prev_tail, w):
    width, cols = w.shape
    rows = lax.broadcasted_iota(jnp.int32, (SUBLANES, cols), 0)
    acc = cur * w[width - 1:width, :]
    for s in range(1, width):
        rolled = pltpu.roll(cur, s, 0)
        head = jnp.where(rows < s, pltpu.roll(prev_tail, s, 0), rolled[0:SUBLANES])
        shifted = jnp.concatenate([head, rolled[SUBLANES:]], axis=0)
        acc = acc + shifted * w[width - 1 - s:width - s, :]
    return acc


def _mixer0_kernel(x_ref, w_in_ref, cw_ref, cb_ref, dtb_ref, alog_ref, dexp_ref, nw_ref, scw_ref, w_out_ref,
                   g_ref, b_ref, e2_ref, tril_ref, o_ref,
                   xb_s, z_s, xbc_s, dt_s, acs_s, ycat_s, prev_xbc_s, prev_sc_s, h_s):
    tm = x_ref.shape[0]
    c = SSD_CHUNK
    gw = SSD_HPG * SSD_HEAD_DIM
    half_rows = tm // 2

    @pl.when(pl.program_id(1) == 0)
    def _():
        prev_xbc_s[...] = jnp.zeros_like(prev_xbc_s)
        prev_sc_s[...] = jnp.zeros_like(prev_sc_s)
        h_s[...] = jnp.zeros_like(h_s)

    xb_s[...] = x_ref[...].astype(BF16)

    def proj(lo, width):
        return _dot(xb_s[...], w_in_ref[:, lo:lo + width])

    def xbc_piece(k):
        cols = slice(k * COL_CHUNK, (k + 1) * COL_CHUNK)
        st = {}

        def mm():
            st["raw"] = proj(_C_XBC + k * COL_CHUNK, COL_CHUNK)

        def vpu():
            raw = st.pop("raw")
            xbc_s[:, cols] = _silu(_causal_conv(raw, prev_xbc_s[:, cols], cw_ref[:, cols]) + cb_ref[:, cols])
            prev_xbc_s[:, cols] = raw[tm - SUBLANES:tm]

        return mm, vpu

    def dt_piece():
        st = {}

        def mm():
            st["raw"] = proj(_C_DT, LANES)

        def vpu():
            pre = st.pop("raw") + dtb_ref[...]
            dt = jnp.maximum(pre, 0.0) + jnp.log1p(jnp.exp(-jnp.abs(pre)))
            dt_s[...] = dt
            lane_row = lax.broadcasted_iota(jnp.int32, (1, LANES), 1)
            da = dt * jnp.where(lane_row < DT_REP * SSD_HEADS, -jnp.exp(alog_ref[...]), 0.0)
            da = jnp.concatenate([da[i * c:(i + 1) * c] for i in range(tm // c)], axis=1)
            acs_s[...] = _dot(tril_ref[...], jnp.concatenate(_split_bf16(da, 3), axis=0))

        return mm, vpu

    def z_piece(k):
        def mm():
            z_s[:, k * COL_CHUNK:(k + 1) * COL_CHUNK] = proj(_C_Z + k * COL_CHUNK, COL_CHUNK)

        return mm

    def sc_piece(k):
        cols = slice(k * COL_CHUNK, (k + 1) * COL_CHUNK)
        st = {}

        def mm():
            st["b"] = proj(_C_SCB + k * COL_CHUNK, COL_CHUNK)
            st["u"] = proj(_C_SCC + k * COL_CHUNK, COL_CHUNK) * proj(_C_SCH + k * COL_CHUNK, COL_CHUNK)

        def vpu():
            u = st.pop("u")
            y_sc = st.pop("b") * _causal_conv(u, prev_sc_s[:, cols], scw_ref[:, cols])
            ycat_s[:, SSD_INNER + k * COL_CHUNK:SSD_INNER + (k + 1) * COL_CHUNK] = y_sc.astype(BF16)
            prev_sc_s[:, cols] = u[tm - SUBLANES:tm]

        return mm, vpu

    def out_piece(r):
        rows = slice(r * half_rows, (r + 1) * half_rows)
        st = {}

        def mm():
            st["mix"] = _dot(ycat_s[rows, :], w_out_ref[...])

        def vpu():
            o_ref[rows, :] = _layer_norm(ALPHA * x_ref[rows, :] + st.pop("mix"), g_ref[...], b_ref[...])

        return mm, vpu

    def ssd_chunk(ci):
        rows = slice(ci * c, (ci + 1) * c)
        st = {}

        def s1():
            lane_sq = lax.broadcasted_iota(jnp.int32, (c, LANES), 1)
            dt = dt_s[rows, :]
            a_cs = acs_s[:, ci * LANES:(ci + 1) * LANES]
            ecs = jnp.exp(a_cs)
            dtdte = dt * jnp.exp(a_cs[c - 1:c, :] - a_cs)
            q = jnp.where(lane_sq < SSD_HEADS, dtdte, ecs)
            ex = _dot(jnp.concatenate(_split_bf16(q, 2), axis=1), e2_ref[...])
            st["dtdte_e"] = ex[:, :SSD_INNER]
            st["ecs_e"] = ex[:, SSD_INNER:]
            st["a_cs"] = a_cs
            st["t"] = jnp.where(lane_sq < SSD_HEADS, a_cs, dt).T

        def s2():
            lane_sq = lax.broadcasted_iota(jnp.int32, (c, LANES), 1)
            causal = lax.broadcasted_iota(jnp.int32, (c, LANES), 0) >= lane_sq
            quad_lane = lax.broadcasted_iota(jnp.int32, (c, 4 * SSD_HEAD_DIM), 1) // SSD_HEAD_DIM
            group_lane = lax.broadcasted_iota(jnp.int32, (c, SSD_GROUPS * SSD_STATE), 1) // SSD_STATE
            a_cs = st.pop("a_cs")
            t = st.pop("t")
            bmf = xbc_s[rows, SSD_INNER:SSD_INNER + SSD_GROUPS * SSD_STATE]
            bm = bmf.astype(BF16)
            cm = xbc_s[rows, SSD_INNER + SSD_GROUPS * SSD_STATE:].astype(BF16)
            st["bm"] = bm
            st["cm"] = cm
            b_diag = jnp.concatenate([jnp.where(group_lane == g, bmf, 0.0) for g in range(SSD_GROUPS)], axis=0)
            cb_all = _dot_nt(cm, b_diag.astype(BF16))
            y_parts = []
            for g in range(SSD_GROUPS):
                cb = cb_all[:, g * c:(g + 1) * c]
                for qd in range(SSD_HPG // 4):
                    h0 = g * SSD_HPG + 4 * qd
                    ms = []
                    for h in range(h0, h0 + 4):
                        seg = jnp.broadcast_to(a_cs[:, h:h + 1], (c, LANES)) - t[h:h + 1, :]
                        dec = jnp.exp(jnp.where(causal, seg, -jnp.inf))
                        ms.append((cb * dec * t[SSD_HEADS + h:SSD_HEADS + h + 1, :]).astype(BF16))
                    xq = xbc_s[rows, h0 * SSD_HEAD_DIM:(h0 + 4) * SSD_HEAD_DIM]
                    rhs = jnp.concatenate([jnp.where(quad_lane == k, xq, 0.0) for k in range(4)], axis=0)
                    y_parts.append(_dot(jnp.concatenate(ms, axis=1), rhs.astype(BF16)))
            st["y"] = jnp.concatenate(y_parts, axis=1)

        def s3():
            bm = st.pop("bm")
            cm = st.pop("cm")
            ecs_e = st["ecs_e"]
            dtdte_e = st.pop("dtdte_e")
            y_off = []
            for g in range(SSD_GROUPS):
                hg = h_s[g]
                y_off.append(_dot(cm[:, g * SSD_STATE:(g + 1) * SSD_STATE], hg.astype(BF16)))
                xd = (xbc_s[rows, g * gw:(g + 1) * gw] * dtdte_e[:, g * gw:(g + 1) * gw]).astype(BF16)
                new = _dot_tn(bm[:, g * SSD_STATE:(g + 1) * SSD_STATE], xd)
                h_s[g] = hg * ecs_e[c - 1:c, g * gw:(g + 1) * gw] + new
            st["y_off"] = jnp.concatenate(y_off, axis=1)

        def s4():
            y = st.pop("y") + st.pop("y_off") * st.pop("ecs_e") + xbc_s[rows, :SSD_INNER] * dexp_ref[...]
            y = y * _silu(z_s[rows, :])
            yn = []
            for g in range(SSD_GROUPS):
                yg = y[:, g * gw:(g + 1) * gw]
                yn.append(yg * lax.rsqrt(jnp.mean(yg * yg, axis=-1, keepdims=True) + RMS_EPS))
            ycat_s[rows, 0:SSD_INNER] = (jnp.concatenate(yn, axis=1) * nw_ref[...]).astype(BF16)

        return [s1, s2, s3, s4]

    head = [xbc_piece(k) for k in range(SSD_XBC // COL_CHUNK)] + [dt_piece()]
    z_tasks = [z_piece(k) for k in range(SSD_INNER // COL_CHUNK)]
    pending = None
    for mm, vpu in head:
        mm()
        if pending is not None:
            if z_tasks:
                z_tasks.pop(0)()
            pending()
        pending = vpu
    pending()

    fillers = z_tasks
    for k in range(SC_DIM // COL_CHUNK):
        fillers.extend(sc_piece(k))
    n_chunks = tm // c
    stages = [s for ci in range(n_chunks) for s in ssd_chunk(ci)]
    out0 = list(out_piece(0))
    first_out0_slot = (n_chunks // 2) * 4
    for i, stage in enumerate(stages):
        if fillers:
            fillers.pop(0)()
        elif i >= first_out0_slot and out0:
            out0.pop(0)()
        stage()
    for task in fillers + out0:
        task()
    mm, vpu = out_piece(1)
    mm()
    vpu()


def _attn_kernel(sink_ref, x_ref, w_qkv_ref, b_qkv_ref, w_o_ref, b_o_ref, g_ref, b_ref, ones_ref, o_ref,
                 q_s, kt_s, v_s, a_s, kprev_s, vprev_s):
    tm = x_ref.shape[0]
    blk = ATTN_BLOCK
    first_tile = pl.program_id(1) == 0
    nq = ATTN_HEADS * ATTN_HEAD_DIM
    nkv2 = 2 * ATTN_KV_HEADS * ATTN_HEAD_DIM
    half = ATTN_HEAD_DIM

    x = x_ref[...]
    qkv = _dot(x.astype(BF16), w_qkv_ref[...]) + b_qkv_ref[...]
    q_s[...] = (qkv[:, :nq] * (ATTN_HEAD_DIM ** -0.5)).astype(BF16)
    @pl.when(first_tile)
    def _():
        kprev_s[...] = jnp.zeros_like(kprev_s)
        vprev_s[...] = jnp.zeros_like(vprev_s)

    kt_s[:, :blk] = kprev_s[...]
    kt = qkv[:, nq:nq + nkv2].T
    lo_r = (lax.broadcasted_iota(jnp.int32, (nkv2, tm), 0) & (LANES - 1)) < half
    for r0, part in ((0, jnp.where(lo_r, kt, 0.0).astype(BF16)), (nkv2, jnp.where(lo_r, 0.0, kt).astype(BF16))):
        kt_s[r0:r0 + nkv2, blk:] = part
        kprev_s[r0:r0 + nkv2, :] = part[:, tm - blk:]
    v_s[:blk, :] = vprev_s[...]
    vd = qkv[:, nq + nkv2:]
    lo = (lax.broadcasted_iota(jnp.int32, (tm, nkv2), 1) & (LANES - 1)) < half
    for c0, part in ((0, jnp.where(lo, vd, 0.0).astype(BF16)), (nkv2, jnp.where(lo, 0.0, vd).astype(BF16))):
        v_s[blk:, c0:c0 + nkv2] = part
        vprev_s[:, c0:c0 + nkv2] = part[tm - blk:]

    qi = lax.broadcasted_iota(jnp.int32, (blk, 2 * blk), 0)
    kj = lax.broadcasted_iota(jnp.int32, (blk, 2 * blk), 1)
    band = (kj > qi) & (kj <= qi + blk)
    band0 = band & (kj >= jnp.where(first_tile, blk, 0))
    lane_o = lax.broadcasted_iota(jnp.int32, (blk, LANES), 1) < half

    def scores(n, j):
        c0 = (j // 2) * LANES
        keys = slice(n * blk, (n + 2) * blk)
        kk = jnp.concatenate([kt_s[c0:c0 + LANES, keys], kt_s[nkv2 + c0:nkv2 + c0 + LANES, keys]], axis=1)
        return _dot(q_s[n * blk:(n + 1) * blk, j * LANES:(j + 1) * LANES], kk)

    def probs(n, j, s):
        valid = band0 if n == 0 else band
        ps = []
        es = []
        for t in range(2):
            sink = sink_ref[2 * j + t]
            sh = jnp.where(valid, s[:, t * 2 * blk:(t + 1) * 2 * blk], -jnp.inf)
            m = jnp.maximum(jnp.max(sh, axis=-1, keepdims=True), sink)
            ps.append(jnp.exp(sh - m).astype(BF16))
            es.append(jnp.exp(sink - m))
        return jnp.concatenate(ps, axis=1), jnp.where(lane_o, es[0], es[1])

    def output(n, j, p, es):
        c0 = (j // 2) * LANES
        keys = slice(n * blk, (n + 2) * blk)
        vv = jnp.concatenate([v_s[keys, c0:c0 + LANES], v_s[keys, nkv2 + c0:nkv2 + c0 + LANES]], axis=0)
        r = _dot(p, jnp.concatenate([vv, ones_ref[...]], axis=1))
        a_s[n * blk:(n + 1) * blk, j * LANES:(j + 1) * LANES] = (r[:, :LANES] / (r[:, LANES:] + es)).astype(BF16)

    steps = [(n, j) for n in range(tm // blk) for j in range(ATTN_HEADS // 2)]
    s_val = {}
    p_val = {}
    for i in range(len(steps) + 2):
        if i < len(steps):
            s_val[i] = scores(*steps[i])
        if 0 <= i - 1 < len(steps):
            p_val[i - 1] = probs(*steps[i - 1], s_val.pop(i - 1))
        if 0 <= i - 2 < len(steps):
            output(*steps[i - 2], *p_val.pop(i - 2))

    mix = _dot(a_s[...], w_o_ref[...]) + b_o_ref[...]
    o_ref[...] = _layer_norm(ALPHA * x + mix, g_ref[...], b_ref[...])


def _ffn_kernel(x_ref, p_ref, wg_ref, wu_ref, wd_ref, g_ref, b_ref, wpg_ref, bpg_ref, wple_ref, o_ref):
    x = x_ref[...]
    xb = x.astype(BF16)
    h = (_silu(_dot(xb, wg_ref[...])) * _dot(xb, wu_ref[...])).astype(BF16)
    x2 = _layer_norm(ALPHA * x + _dot(h, wd_ref[...]), g_ref[...], b_ref[...])
    gate = _sigmoid(_dot(x2.astype(BF16), wpg_ref[...]) + bpg_ref[...])
    pe = _dot(p_ref[...].astype(BF16), wple_ref[...])
    o_ref[...] = x2 + pe * gate


def _resident(shape):
    return pl.BlockSpec(shape, lambda *_: (0,) * len(shape), pipeline_mode=pl.Buffered(1))


def _row(v):
    return v.reshape(1, -1).astype(F32)


def _pad_lanes(v, width):
    return jnp.pad(v, ((0, 0), (0, width - v.shape[1])))


def _mixer0(x2d, batch, w_in, conv_w, conv_b, dt_bias, a_log, d_skip, norm_w, sc_conv_w, w_out, ln_g, ln_b):
    t = x2d.shape[0]
    n_l = t // batch // TM_MIX0
    o1 = SSD_INNER
    o2 = o1 + SSD_XBC
    o3 = o2 + SSD_HEADS
    w_in_p = jnp.concatenate(
        [w_in[:, :o2], w_in[:, o3:], _pad_lanes(jnp.tile(w_in[:, o2:o3], (1, DT_REP)), LANES)], axis=1).astype(BF16)
    dtb = _pad_lanes(jnp.tile(_row(dt_bias), (1, DT_REP)), LANES)
    alog = _pad_lanes(jnp.tile(_row(a_log), (1, DT_REP)), LANES)
    dexp = jnp.repeat(_row(d_skip), SSD_HEAD_DIM, axis=1)
    r = jnp.arange(LANES)[:, None]
    col = jnp.arange(DT_REP * SSD_INNER)[None, :]
    e1 = ((r < DT_REP * SSD_HEADS) & (col // SSD_HEAD_DIM == r)).astype(BF16)
    e2 = jnp.concatenate([e1, e1], axis=0)
    tri = (jnp.arange(SSD_CHUNK)[:, None] >= jnp.arange(SSD_CHUNK)[None, :]).astype(BF16)
    tril3 = jnp.concatenate([tri, tri, tri], axis=1)

    tok = lambda width: pl.BlockSpec((TM_MIX0, width), lambda b, l: (b * n_l + l, 0))
    return pl.pallas_call(
        _mixer0_kernel,
        grid=(batch, n_l),
        in_specs=[tok(D_MODEL), _resident((D_MODEL, MIX_IN_P)), _resident((SSD_CONV, SSD_XBC)),
                  _resident((1, SSD_XBC)), _resident((1, LANES)), _resident((1, LANES)),
                  _resident((1, SSD_INNER)), _resident((1, SSD_INNER)), _resident((SC_CONV, SC_DIM)),
                  _resident((MIX_OUT, D_MODEL)), _resident((1, D_MODEL)), _resident((1, D_MODEL)),
                  _resident(e2.shape), _resident(tril3.shape)],
        out_specs=tok(D_MODEL),
        out_shape=jax.ShapeDtypeStruct((t, D_MODEL), F32),
        scratch_shapes=[pltpu.VMEM((TM_MIX0, D_MODEL), BF16),
                        pltpu.VMEM((TM_MIX0, SSD_INNER), F32), pltpu.VMEM((TM_MIX0, SSD_XBC), F32),
                        pltpu.VMEM((TM_MIX0, LANES), F32), pltpu.VMEM((SSD_CHUNK, TM_MIX0), F32),
                        pltpu.VMEM((TM_MIX0, MIX_OUT), BF16),
                        pltpu.VMEM((SUBLANES, SSD_XBC), F32), pltpu.VMEM((SUBLANES, SC_DIM), F32),
                        pltpu.VMEM((SSD_GROUPS, SSD_STATE, SSD_HPG * SSD_HEAD_DIM), F32)],
        compiler_params=pltpu.CompilerParams(dimension_semantics=("arbitrary", "arbitrary"),
                                             vmem_limit_bytes=VMEM_LIMIT_BYTES),
        name="mixer0_ssd_shortconv",
    )(x2d, w_in_p, conv_w.astype(F32), _row(conv_b), dtb, alog, dexp, _row(norm_w), sc_conv_w.astype(F32),
      w_out.astype(BF16), _row(ln_g), _row(ln_b), e2, tril3)


def _attn(x2d, batch, w_qkv, b_qkv, sinks, w_o, b_o, ln_g, ln_b):
    t = x2d.shape[0]
    n_l = t // batch // TM_ATTN
    nq = ATTN_HEADS * ATTN_HEAD_DIM
    nkv = ATTN_KV_HEADS * ATTN_HEAD_DIM

    def dup_heads(m):
        lead = m.shape[:-1]
        m = m.reshape(lead + (ATTN_KV_HEADS, 1, ATTN_HEAD_DIM))
        return jnp.broadcast_to(m, lead + (ATTN_KV_HEADS, 2, ATTN_HEAD_DIM)).reshape(lead + (2 * nkv,))

    w_p = jnp.concatenate([w_qkv[:, :nq], dup_heads(w_qkv[:, nq:nq + nkv]), dup_heads(w_qkv[:, nq + nkv:])],
                          axis=1).astype(BF16)
    b_p = _row(jnp.concatenate([b_qkv[:nq], dup_heads(b_qkv[nq:nq + nkv]), dup_heads(b_qkv[nq + nkv:])]))
    width = nq + 4 * nkv
    ones_blk = (jnp.arange(4 * ATTN_BLOCK)[:, None] // (2 * ATTN_BLOCK) == jnp.arange(LANES)[None, :] // ATTN_HEAD_DIM
                ).astype(BF16)
    tok = pl.BlockSpec((TM_ATTN, D_MODEL), lambda b, l: (b * n_l + l, 0))
    return pl.pallas_call(
        _attn_kernel,
        grid=(batch, n_l),
        in_specs=[pl.BlockSpec(memory_space=pltpu.SMEM), tok, _resident((D_MODEL, width)), _resident((1, width)),
                  _resident((nq, D_MODEL)), _resident((1, D_MODEL)), _resident((1, D_MODEL)),
                  _resident((1, D_MODEL)), _resident(ones_blk.shape)],
        out_specs=tok,
        out_shape=jax.ShapeDtypeStruct((t, D_MODEL), F32),
        scratch_shapes=[pltpu.VMEM((TM_ATTN, nq), BF16), pltpu.VMEM((4 * nkv, TM_ATTN + ATTN_BLOCK), BF16),
                        pltpu.VMEM((TM_ATTN + ATTN_BLOCK, 4 * nkv), BF16), pltpu.VMEM((TM_ATTN, nq), BF16),
                        pltpu.VMEM((4 * nkv, ATTN_BLOCK), BF16), pltpu.VMEM((ATTN_BLOCK, 4 * nkv), BF16)],
        compiler_params=pltpu.CompilerParams(dimension_semantics=("arbitrary", "arbitrary"),
                                             vmem_limit_bytes=VMEM_LIMIT_BYTES),
        name="mixer1_swa",
    )(sinks.astype(F32), x2d, w_p, b_p, w_o.astype(BF16), _row(b_o), _row(ln_g), _row(ln_b), ones_blk)


def _ffn(x2d, layer, p3d, w_gate, w_up, w_down, ln_g, ln_b, w_ple, w_ple_gate, b_ple_gate):
    t = x2d.shape[0]
    tok = lambda width: pl.BlockSpec((TM_FFN, width), lambda i: (i, 0))

    def layer_resident(rows, cols):
        return pl.BlockSpec((None, rows, cols), lambda i: (layer, 0, 0), pipeline_mode=pl.Buffered(1))

    return pl.pallas_call(
        _ffn_kernel,
        grid=(t // TM_FFN,),
        in_specs=[tok(D_MODEL), pl.BlockSpec((None, TM_FFN, PLE_DIM), lambda i: (layer, i, 0)),
                  layer_resident(D_MODEL, D_FF), layer_resident(D_MODEL, D_FF), layer_resident(D_FF, D_MODEL),
                  _resident((1, D_MODEL)), _resident((1, D_MODEL)),
                  layer_resident(D_MODEL, D_MODEL), _resident((1, D_MODEL)), layer_resident(PLE_DIM, D_MODEL)],
        out_specs=tok(D_MODEL),
        out_shape=jax.ShapeDtypeStruct((t, D_MODEL), F32),
        compiler_params=pltpu.CompilerParams(dimension_semantics=("arbitrary",),
                                             vmem_limit_bytes=VMEM_LIMIT_BYTES),
        name="ffn_ple",
    )(x2d, p3d, w_gate, w_up, w_down, _row(ln_g[layer]), _row(ln_b[layer]),
      w_ple_gate, _row(b_ple_gate[layer]), w_ple)


def kernel(x, p, w_in_mix, ssd_conv_w, ssd_conv_b, ssd_dt_bias, ssd_a_log, ssd_d, ssd_norm_w, sc_conv_w, w_out_mix, w_qkv, b_qkv, attn_sinks, w_o, b_o, ln_mix_g, ln_mix_b, w_ffn_gate, w_ffn_up, w_ffn_down, ln_ffn_g, ln_ffn_b, w_ple, w_ple_gate, b_ple_gate):
    batch, seq, d = x.shape
    assert d == D_MODEL and seq % TM_MIX0 == 0 and seq % TM_ATTN == 0 and (batch * seq) % TM_FFN == 0
    assert p.shape == (DEPTH, batch, seq, PLE_DIM)
    h = x.reshape(batch * seq, d)
    p2d = p.reshape(DEPTH, batch * seq, PLE_DIM)

    ffn_w = [w.astype(BF16) for w in (w_ffn_gate, w_ffn_up, w_ffn_down)]
    ple_w = w_ple.astype(BF16)
    ple_gate_w = w_ple_gate.astype(BF16)

    def ffn(h, i):
        return _ffn(h, i, p2d, *ffn_w, ln_ffn_g, ln_ffn_b, ple_w, ple_gate_w, b_ple_gate)

    h = _mixer0(h, batch, w_in_mix[0], ssd_conv_w[0], ssd_conv_b[0], ssd_dt_bias[0], ssd_a_log[0], ssd_d[0],
                ssd_norm_w[0], sc_conv_w[0], w_out_mix[0], ln_mix_g[0], ln_mix_b[0])
    h = ffn(h, 0)
    h = _attn(h, batch, w_qkv[0], b_qkv[0], attn_sinks[0], w_o[0], b_o[0], ln_mix_g[1], ln_mix_b[1])
    h = ffn(h, 1)
    return h.reshape(batch, seq, d)
```

```python
import functools

import jax
import jax.numpy as jnp
from jax import lax
from jax.experimental import pallas as pl
from jax.experimental.pallas import tpu as pltpu

F32 = jnp.float32
BF16 = jnp.bfloat16

D_MODEL = 1024
DEPTH = 2
PLE_DIM = 256
SSD_HEADS = 16
SSD_HEAD_DIM = 64
SSD_INNER = SSD_HEADS * SSD_HEAD_DIM
SSD_GROUPS = 2
SSD_HPG = SSD_HEADS // SSD_GROUPS
SSD_STATE = 128
SSD_CONV = 4
SSD_CHUNK = 128
SSD_XBC = SSD_INNER + 2 * SSD_GROUPS * SSD_STATE
SC_DIM = 1024
SC_CONV = 3
MIX_OUT = SSD_INNER + SC_DIM
ATTN_HEADS = 16
ATTN_KV_HEADS = 4
ATTN_HEAD_DIM = 64
ATTN_BLOCK = 128
D_FF = ((8 * D_MODEL + 3 * 256 - 1) // (3 * 256)) * 256
ALPHA = (2 * DEPTH) ** 0.25
LN_EPS = 1e-5
RMS_EPS = 1e-5

LANES = 128
SUBLANES = 8
V7X_VMEM_BYTES = 64 * 1024 * 1024
VMEM_LIMIT_BYTES = V7X_VMEM_BYTES - 8 * 1024 * 1024

TM_MIX0 = 512
TM_ATTN = 512
TM_FFN = 1024
FFN_ROWS = 256

_C_Z = 0
_C_XBC = _C_Z + SSD_INNER
_C_SCB = _C_XBC + SSD_XBC
_C_SCC = _C_SCB + SC_DIM
_C_SCH = _C_SCC + SC_DIM
_C_DT = _C_SCH + SC_DIM
MIX_IN_P = _C_DT + LANES
DT_REP = 2
COL_CHUNK = 256


def _dot(a, b):
    return jnp.dot(a, b, preferred_element_type=F32)


def _dot_nt(a, b):
    return lax.dot_general(a, b, (((1,), (1,)), ((), ())), preferred_element_type=F32)


def _dot_tn(a, b):
    return lax.dot_general(a, b, (((0,), (0,)), ((), ())), preferred_element_type=F32)


def _sigmoid(x):
    return 1.0 / (1.0 + jnp.exp(-x))


def _silu(x):
    return x * _sigmoid(x)


def _layer_norm(r, g, b):
    mu = jnp.mean(r, axis=-1, keepdims=True)
    xc = r - mu
    var = jnp.mean(xc * xc, axis=-1, keepdims=True)
    return xc * lax.rsqrt(var + LN_EPS) * g + b


def _split_bf16(v, parts):
    out = []
    rem = v
    for i in range(parts):
        p = rem.astype(BF16)
        out.append(p)
        if i + 1 < parts:
            rem = rem - p.astype(F32)
    return out


def _causal_conv(cur, prev_tail, w):
    width, cols = w.shape
    rows = lax.broadcasted_iota(jnp.int32, (SUBLANES, cols), 0)
    acc = cur * w[width - 1:width, :]
    for s in range(1, width):
        rolled = pltpu.roll(cur, s, 0)
        head = jnp.where(rows < s, pltpu.roll(prev_tail, s, 0), rolled[0:SUBLANES])
        shifted = jnp.concatenate([head, rolled[SUBLANES:]], axis=0)
        acc = acc + shifted * w[width - 1 - s:width - s, :]
    return acc


def _mixer0_kernel(x_ref, w_in_ref, cw_ref, cb_ref, dtb_ref, alog_ref, dexp_ref, nw_ref, scw_ref, w_out_ref,
                   g_ref, b_ref, e2_ref, tril_ref, o_ref,
                   xb_s, z_s, xbc_s, dt_s, acs_s, ycat_s, prev_xbc_s, prev_sc_s, h_s):
    tm = x_ref.shape[0]
    c = SSD_CHUNK
    gw = SSD_HPG * SSD_HEAD_DIM

    @pl.when(pl.program_id(1) == 0)
    def _():
        prev_xbc_s[...] = jnp.zeros_like(prev_xbc_s)
        prev_sc_s[...] = jnp.zeros_like(prev_sc_s)
        h_s[...] = jnp.zeros_like(h_s)

    xb_s[...] = x_ref[...].astype(BF16)

    def proj(lo, width):
        return _dot(xb_s[...], w_in_ref[:, lo:lo + width])

    def xbc_piece(k):
        cols = slice(k * COL_CHUNK, (k + 1) * COL_CHUNK)
        st = {}

        def mm():
            st["raw"] = proj(_C_XBC + k * COL_CHUNK, COL_CHUNK)

        def vpu():
            raw = st.pop("raw")
            xbc_s[:, cols] = _silu(_causal_conv(raw, prev_xbc_s[:, cols], cw_ref[:, cols]) + cb_ref[:, cols])
            prev_xbc_s[:, cols] = raw[tm - SUBLANES:tm]

        return mm, vpu

    def dt_piece():
        st = {}

        def mm():
            st["raw"] = proj(_C_DT, LANES)

        def vpu():
            pre = st.pop("raw") + dtb_ref[...]
            dt = jnp.maximum(pre, 0.0) + jnp.log1p(jnp.exp(-jnp.abs(pre)))
            dt_s[...] = dt
            lane_row = lax.broadcasted_iota(jnp.int32, (1, LANES), 1)
            da = dt * jnp.where(lane_row < DT_REP * SSD_HEADS, -jnp.exp(alog_ref[...]), 0.0)
            da = jnp.concatenate([da[i * c:(i + 1) * c] for i in range(tm // c)], axis=1)
            acs_s[...] = _dot(tril_ref[...], jnp.concatenate(_split_bf16(da, 3), axis=0))

        return mm, vpu

    def z_piece(k):
        def mm():
            z_s[:, k * COL_CHUNK:(k + 1) * COL_CHUNK] = proj(_C_Z + k * COL_CHUNK, COL_CHUNK)

        return mm

    def sc_piece(k):
        cols = slice(k * COL_CHUNK, (k + 1) * COL_CHUNK)
        st = {}

        def mm_b():
            st["b"] = proj(_C_SCB + k * COL_CHUNK, COL_CHUNK)

        def mm_c():
            st["c"] = proj(_C_SCC + k * COL_CHUNK, COL_CHUNK)

        def mm_h():
            st["u"] = st.pop("c") * proj(_C_SCH + k * COL_CHUNK, COL_CHUNK)

        def vpu():
            u = st.pop("u")
            y_sc = st.pop("b") * _causal_conv(u, prev_sc_s[:, cols], scw_ref[:, cols])
            ycat_s[:, SSD_INNER + k * COL_CHUNK:SSD_INNER + (k + 1) * COL_CHUNK] = y_sc.astype(BF16)
            prev_sc_s[:, cols] = u[tm - SUBLANES:tm]

        return mm_b, mm_c, mm_h, vpu

    def out_piece(i):
        rows = slice(i * c, (i + 1) * c)
        st = {}

        def sc_half():
            st["mix"] = _dot(ycat_s[rows, SSD_INNER:], w_out_ref[SSD_INNER:, :])

        def ssd_half():
            mix = st.pop("mix") + _dot(ycat_s[rows, :SSD_INNER], w_out_ref[:SSD_INNER, :])
            o_ref[rows, :] = _layer_norm(ALPHA * x_ref[rows, :] + mix, g_ref[...], b_ref[...])

        return sc_half, ssd_half

    def ssd_chunk(ci):
        rows = slice(ci * c, (ci + 1) * c)
        st = {}

        def s1():
            lane_sq = lax.broadcasted_iota(jnp.int32, (c, LANES), 1)
            dt = dt_s[rows, :]
            a_cs = acs_s[:, ci * LANES:(ci + 1) * LANES]
            ecs = jnp.exp(a_cs)
            dtdte = dt * jnp.exp(a_cs[c - 1:c, :] - a_cs)
            q = jnp.where(lane_sq < SSD_HEADS, dtdte, ecs)
            ex = _dot(jnp.concatenate(_split_bf16(q, 2), axis=1), e2_ref[...])
            st["dtdte_e"] = ex[:, :SSD_INNER]
            st["ecs_e"] = ex[:, SSD_INNER:]
            st["a_cs"] = a_cs
            st["t"] = jnp.where(lane_sq < SSD_HEADS, a_cs, dt).T

        def s2():
            group_lane = lax.broadcasted_iota(jnp.int32, (c, SSD_GROUPS * SSD_STATE), 1) // SSD_STATE
            bmf = xbc_s[rows, SSD_INNER:SSD_INNER + SSD_GROUPS * SSD_STATE]
            bm = bmf.astype(BF16)
            cm = xbc_s[rows, SSD_INNER + SSD_GROUPS * SSD_STATE:].astype(BF16)
            st["bm"] = bm
            st["cm"] = cm
            b_diag = jnp.concatenate([jnp.where(group_lane == g, bmf, 0.0) for g in range(SSD_GROUPS)], axis=0)
            st["cb"] = _dot_nt(cm, b_diag.astype(BF16))
            st["y"] = []

        def s2_quad(qi):
            lane_sq = lax.broadcasted_iota(jnp.int32, (c, LANES), 1)
            causal = lax.broadcasted_iota(jnp.int32, (c, LANES), 0) >= lane_sq
            quad_lane = lax.broadcasted_iota(jnp.int32, (c, 4 * SSD_HEAD_DIM), 1) // SSD_HEAD_DIM
            g = (4 * qi) // SSD_HPG
            h0 = 4 * qi
            a_cs = st["a_cs"]
            t = st["t"]
            cb = st["cb"][:, g * c:(g + 1) * c]
            ms = []
            for h in range(h0, h0 + 4):
                seg = jnp.broadcast_to(a_cs[:, h:h + 1], (c, LANES)) - t[h:h + 1, :]
                dec = jnp.exp(jnp.where(causal, seg, -jnp.inf))
                ms.append((cb * dec * t[SSD_HEADS + h:SSD_HEADS + h + 1, :]).astype(BF16))
            xq = xbc_s[rows, h0 * SSD_HEAD_DIM:(h0 + 4) * SSD_HEAD_DIM]
            rhs = jnp.concatenate([jnp.where(quad_lane == k, xq, 0.0) for k in range(4)], axis=0)
            st["y"].append(_dot(jnp.concatenate(ms, axis=1), rhs.astype(BF16)))

        def s3():
            for key in ("a_cs", "t", "cb"):
                st.pop(key)
            bm = st.pop("bm")
            cm = st.pop("cm")
            ecs_e = st["ecs_e"]
            dtdte_e = st.pop("dtdte_e")
            y_off = []
            for g in range(SSD_GROUPS):
                hg = h_s[g]
                y_off.append(_dot(cm[:, g * SSD_STATE:(g + 1) * SSD_STATE], hg.astype(BF16)))
                xd = (xbc_s[rows, g * gw:(g + 1) * gw] * dtdte_e[:, g * gw:(g + 1) * gw]).astype(BF16)
                new = _dot_tn(bm[:, g * SSD_STATE:(g + 1) * SSD_STATE], xd)
                h_s[g] = hg * ecs_e[c - 1:c, g * gw:(g + 1) * gw] + new
            st["y_off"] = jnp.concatenate(y_off, axis=1)

        def s4():
            y = jnp.concatenate(st.pop("y"), axis=1) + st.pop("y_off") * st.pop("ecs_e")
            y = y + xbc_s[rows, :SSD_INNER] * dexp_ref[...]
            y = y * _silu(z_s[rows, :])
            yn = []
            for g in range(SSD_GROUPS):
                yg = y[:, g * gw:(g + 1) * gw]
                yn.append(yg * lax.rsqrt(jnp.mean(yg * yg, axis=-1, keepdims=True) + RMS_EPS))
            ycat_s[rows, 0:SSD_INNER] = (jnp.concatenate(yn, axis=1) * nw_ref[...]).astype(BF16)

        quads = [functools.partial(s2_quad, qi) for qi in range(SSD_HEADS // 4)]
        return [s1, s2] + quads + [s3, s4]

    head = [xbc_piece(k) for k in range(SSD_XBC // COL_CHUNK)] + [dt_piece()]
    z_tasks = [z_piece(k) for k in range(SSD_INNER // COL_CHUNK)]
    pending = None
    for mm, vpu in head:
        mm()
        if pending is not None:
            if z_tasks:
                z_tasks.pop(0)()
            pending()
        pending = vpu
    pending()

    n_chunks = tm // c
    fillers = [(0, t) for t in z_tasks]
    for k in range(SC_DIM // COL_CHUNK):
        fillers.extend((0, t) for t in sc_piece(k))
    outs = [out_piece(i) for i in range(n_chunks)]
    fillers.extend((0, sc_half) for sc_half, _ in outs)
    per_chunk = len(ssd_chunk(0))
    fillers.extend((per_chunk * (i + 1), ssd_half) for i, (_, ssd_half) in enumerate(outs))
    stages = [s for ci in range(n_chunks) for s in ssd_chunk(ci)]
    n_fill = len(fillers)
    for i, stage in enumerate(stages):
        want = ((i + 1) * n_fill) // len(stages) - (n_fill - len(fillers))
        while want > 0 and fillers and fillers[0][0] <= i:
            fillers.pop(0)[1]()
            want -= 1
        stage()
    for _, task in fillers:
        task()


def _attn_kernel(sink_ref, x_ref, w_qkv_ref, b_qkv_ref, w_o_ref, b_o_ref, g_ref, b_ref, ones_ref, o_ref,
                 xb_s, q_s, kt_s, v_s, a_s, kprev_s, vprev_s):
    tm = x_ref.shape[0]
    blk = ATTN_BLOCK
    first_tile = pl.program_id(1) == 0
    nq = ATTN_HEADS * ATTN_HEAD_DIM
    nkv2 = 2 * ATTN_KV_HEADS * ATTN_HEAD_DIM
    half = ATTN_HEAD_DIM

    @pl.when(first_tile)
    def _():
        kprev_s[...] = jnp.zeros_like(kprev_s)
        vprev_s[...] = jnp.zeros_like(vprev_s)

    kt_s[:, :blk] = kprev_s[...]
    v_s[:blk, :] = vprev_s[...]
    xb_s[...] = x_ref[...].astype(BF16)

    def proj_piece(k):
        lo = k * COL_CHUNK
        st = {}

        def mm():
            st["v"] = _dot(xb_s[...], w_qkv_ref[:, lo:lo + COL_CHUNK]) + b_qkv_ref[:, lo:lo + COL_CHUNK]

        def q_vpu():
            q_s[:, lo:lo + COL_CHUNK] = (st.pop("v") * (ATTN_HEAD_DIM ** -0.5)).astype(BF16)

        def k_vpu():
            r0 = lo - nq
            kt = st.pop("v").T
            lo_r = (lax.broadcasted_iota(jnp.int32, (COL_CHUNK, tm), 0) & (LANES - 1)) < half
            for base, part in ((0, jnp.where(lo_r, kt, 0.0)), (nkv2, jnp.where(lo_r, 0.0, kt))):
                part = part.astype(BF16)
                kt_s[base + r0:base + r0 + COL_CHUNK, blk:] = part
                kprev_s[base + r0:base + r0 + COL_CHUNK, :] = part[:, tm - blk:]

        def v_vpu():
            c0 = lo - nq - nkv2
            vd = st.pop("v")
            lo_c = (lax.broadcasted_iota(jnp.int32, (tm, COL_CHUNK), 1) & (LANES - 1)) < half
            for base, part in ((0, jnp.where(lo_c, vd, 0.0)), (nkv2, jnp.where(lo_c, 0.0, vd))):
                part = part.astype(BF16)
                v_s[blk:, base + c0:base + c0 + COL_CHUNK] = part
                vprev_s[:, base + c0:base + c0 + COL_CHUNK] = part[tm - blk:]

        vpu = q_vpu if lo < nq else (k_vpu if lo < nq + nkv2 else v_vpu)
        return mm, vpu

    pending = None
    for k in range((nq + 2 * nkv2) // COL_CHUNK):
        mm, vpu = proj_piece(k)
        mm()
        if pending is not None:
            pending()
        pending = vpu
    pending()

    qi = lax.broadcasted_iota(jnp.int32, (blk, 2 * blk), 0)
    kj = lax.broadcasted_iota(jnp.int32, (blk, 2 * blk), 1)
    band = (kj > qi) & (kj <= qi + blk)
    band0 = band & (kj >= jnp.where(first_tile, blk, 0))
    lane_o = lax.broadcasted_iota(jnp.int32, (blk, LANES), 1) < half

    def scores(n, j):
        c0 = (j // 2) * LANES
        keys = slice(n * blk, (n + 2) * blk)
        kk = jnp.concatenate([kt_s[c0:c0 + LANES, keys], kt_s[nkv2 + c0:nkv2 + c0 + LANES, keys]], axis=1)
        return _dot(q_s[n * blk:(n + 1) * blk, j * LANES:(j + 1) * LANES], kk)

    def probs(n, j, s):
        valid = band0 if n == 0 else band
        ps = []
        es = []
        for t in range(2):
            sink = sink_ref[2 * j + t]
            sh = jnp.where(valid, s[:, t * 2 * blk:(t + 1) * 2 * blk], -jnp.inf)
            m = jnp.maximum(jnp.max(sh, axis=-1, keepdims=True), sink)
            ps.append(jnp.exp(sh - m).astype(BF16))
            es.append(jnp.exp(sink - m))
        return jnp.concatenate(ps, axis=1), jnp.where(lane_o, es[0], es[1])

    def output(n, j, p, es):
        c0 = (j // 2) * LANES
        keys = slice(n * blk, (n + 2) * blk)
        vv = jnp.concatenate([v_s[keys, c0:c0 + LANES], v_s[keys, nkv2 + c0:nkv2 + c0 + LANES]], axis=0)
        r = _dot(p, jnp.concatenate([vv, ones_ref[...]], axis=1))
        a_s[n * blk:(n + 1) * blk, j * LANES:(j + 1) * LANES] = (r[:, :LANES] / (r[:, LANES:] + es)).astype(BF16)

    def out_block(n):
        rows = slice(n * blk, (n + 1) * blk)
        st = {}

        def mm():
            st["mix"] = _dot(a_s[rows, :], w_o_ref[...]) + b_o_ref[...]

        def norm():
            o_ref[rows, :] = _layer_norm(ALPHA * x_ref[rows, :] + st.pop("mix"), g_ref[...], b_ref[...])

        return mm, norm

    steps = [(n, j) for n in range(tm // blk) for j in range(ATTN_HEADS // 2)]
    pairs = ATTN_HEADS // 2
    extra = {}
    for n in range(tm // blk - 1):
        mm, norm = out_block(n)
        extra[pairs * (n + 1) + 3] = mm
        extra[pairs * (n + 1) + 6] = norm
    s_val = {}
    p_val = {}
    for i in range(len(steps) + 2):
        if i < len(steps):
            s_val[i] = scores(*steps[i])
        if i in extra:
            extra[i]()
        if 0 <= i - 1 < len(steps):
            p_val[i - 1] = probs(*steps[i - 1], s_val.pop(i - 1))
        if 0 <= i - 2 < len(steps):
            output(*steps[i - 2], *p_val.pop(i - 2))
    for task in out_block(tm // blk - 1):
        task()


def _ffn_kernel(x_ref, p_ref, wg_ref, wu_ref, wd_ref, g_ref, b_ref, wpg_ref, bpg_ref, wple_ref, o_ref):
    tm = x_ref.shape[0]
    n_blk = tm // FFN_ROWS

    def blk(i):
        rows = slice(i * FFN_ROWS, (i + 1) * FFN_ROWS)
        st = {}

        def gate_up():
            xb = x_ref[rows, :].astype(BF16)
            st["g"] = _dot(xb, wg_ref[...])
            st["u"] = _dot(xb, wu_ref[...])

        def act():
            st["h"] = (_silu(st.pop("g")) * st.pop("u")).astype(BF16)

        def down():
            st["y"] = _dot(st.pop("h"), wd_ref[...])

        def norm():
            st["x2"] = _layer_norm(ALPHA * x_ref[rows, :] + st.pop("y"), g_ref[...], b_ref[...])

        def ple():
            st["gate"] = _dot(st["x2"].astype(BF16), wpg_ref[...])
            st["pe"] = _dot(p_ref[rows, :].astype(BF16), wple_ref[...])

        def out():
            o_ref[rows, :] = st.pop("x2") + st.pop("pe") * _sigmoid(st.pop("gate") + bpg_ref[...])

        return [gate_up, act, down, norm, ple, out]

    stages = [blk(i) for i in range(n_blk)]
    n_st = len(stages[0])
    for step in range(0, n_st + 2 * (n_blk - 1), 2):
        for i in range(n_blk):
            k = step - 2 * i
            if 0 <= k < n_st:
                stages[i][k]()
        for i in range(n_blk):
            k = step - 2 * i + 1
            if 0 <= k < n_st:
                stages[i][k]()


def _resident(shape):
    return pl.BlockSpec(shape, lambda *_: (0,) * len(shape), pipeline_mode=pl.Buffered(1))


def _row(v):
    return v.reshape(1, -1).astype(F32)


def _pad_lanes(v, width):
    return jnp.pad(v, ((0, 0), (0, width - v.shape[1])))


def _mixer0(x2d, batch, w_in, conv_w, conv_b, dt_bias, a_log, d_skip, norm_w, sc_conv_w, w_out, ln_g, ln_b):
    t = x2d.shape[0]
    n_l = t // batch // TM_MIX0
    o1 = SSD_INNER
    o2 = o1 + SSD_XBC
    o3 = o2 + SSD_HEADS
    w_in_p = jnp.concatenate(
        [w_in[:, :o2], w_in[:, o3:], _pad_lanes(jnp.tile(w_in[:, o2:o3], (1, DT_REP)), LANES)], axis=1).astype(BF16)
    dtb = _pad_lanes(jnp.tile(_row(dt_bias), (1, DT_REP)), LANES)
    alog = _pad_lanes(jnp.tile(_row(a_log), (1, DT_REP)), LANES)
    dexp = jnp.repeat(_row(d_skip), SSD_HEAD_DIM, axis=1)
    r = jnp.arange(LANES)[:, None]
    col = jnp.arange(DT_REP * SSD_INNER)[None, :]
    e1 = ((r < DT_REP * SSD_HEADS) & (col // SSD_HEAD_DIM == r)).astype(BF16)
    e2 = jnp.concatenate([e1, e1], axis=0)
    tri = (jnp.arange(SSD_CHUNK)[:, None] >= jnp.arange(SSD_CHUNK)[None, :]).astype(BF16)
    tril3 = jnp.concatenate([tri, tri, tri], axis=1)

    tok = lambda width: pl.BlockSpec((TM_MIX0, width), lambda b, l: (b * n_l + l, 0))
    return pl.pallas_call(
        _mixer0_kernel,
        grid=(batch, n_l),
        in_specs=[tok(D_MODEL), _resident((D_MODEL, MIX_IN_P)), _resident((SSD_CONV, SSD_XBC)),
                  _resident((1, SSD_XBC)), _resident((1, LANES)), _resident((1, LANES)),
                  _resident((1, SSD_INNER)), _resident((1, SSD_INNER)), _resident((SC_CONV, SC_DIM)),
                  _resident((MIX_OUT, D_MODEL)), _resident((1, D_MODEL)), _resident((1, D_MODEL)),
                  _resident(e2.shape), _resident(tril3.shape)],
        out_specs=tok(D_MODEL),
        out_shape=jax.ShapeDtypeStruct((t, D_MODEL), F32),
        scratch_shapes=[pltpu.VMEM((TM_MIX0, D_MODEL), BF16),
                        pltpu.VMEM((TM_MIX0, SSD_INNER), F32), pltpu.VMEM((TM_MIX0, SSD_XBC), F32),
                        pltpu.VMEM((TM_MIX0, LANES), F32), pltpu.VMEM((SSD_CHUNK, TM_MIX0), F32),
                        pltpu.VMEM((TM_MIX0, MIX_OUT), BF16),
                        pltpu.VMEM((SUBLANES, SSD_XBC), F32), pltpu.VMEM((SUBLANES, SC_DIM), F32),
                        pltpu.VMEM((SSD_GROUPS, SSD_STATE, SSD_HPG * SSD_HEAD_DIM), F32)],
        compiler_params=pltpu.CompilerParams(dimension_semantics=("arbitrary", "arbitrary"),
                                             vmem_limit_bytes=VMEM_LIMIT_BYTES),
        name="mixer0_ssd_shortconv",
    )(x2d, w_in_p, conv_w.astype(F32), _row(conv_b), dtb, alog, dexp, _row(norm_w), sc_conv_w.astype(F32),
      w_out.astype(BF16), _row(ln_g), _row(ln_b), e2, tril3)


def _attn(x2d, batch, w_qkv, b_qkv, sinks, w_o, b_o, ln_g, ln_b):
    t = x2d.shape[0]
    n_l = t // batch // TM_ATTN
    nq = ATTN_HEADS * ATTN_HEAD_DIM
    nkv = ATTN_KV_HEADS * ATTN_HEAD_DIM

    def dup_heads(m):
        lead = m.shape[:-1]
        m = m.reshape(lead + (ATTN_KV_HEADS, 1, ATTN_HEAD_DIM))
        return jnp.broadcast_to(m, lead + (ATTN_KV_HEADS, 2, ATTN_HEAD_DIM)).reshape(lead + (2 * nkv,))

    w_p = jnp.concatenate([w_qkv[:, :nq], dup_heads(w_qkv[:, nq:nq + nkv]), dup_heads(w_qkv[:, nq + nkv:])],
                          axis=1).astype(BF16)
    b_p = _row(jnp.concatenate([b_qkv[:nq], dup_heads(b_qkv[nq:nq + nkv]), dup_heads(b_qkv[nq + nkv:])]))
    width = nq + 4 * nkv
    ones_blk = (jnp.arange(4 * ATTN_BLOCK)[:, None] // (2 * ATTN_BLOCK) == jnp.arange(LANES)[None, :] // ATTN_HEAD_DIM
                ).astype(BF16)
    tok = pl.BlockSpec((TM_ATTN, D_MODEL), lambda b, l: (b * n_l + l, 0))
    return pl.pallas_call(
        _attn_kernel,
        grid=(batch, n_l),
        in_specs=[pl.BlockSpec(memory_space=pltpu.SMEM), tok, _resident((D_MODEL, width)), _resident((1, width)),
                  _resident((nq, D_MODEL)), _resident((1, D_MODEL)), _resident((1, D_MODEL)),
                  _resident((1, D_MODEL)), _resident(ones_blk.shape)],
        out_specs=tok,
        out_shape=jax.ShapeDtypeStruct((t, D_MODEL), F32),
        scratch_shapes=[pltpu.VMEM((TM_ATTN, D_MODEL), BF16),
                        pltpu.VMEM((TM_ATTN, nq), BF16), pltpu.VMEM((4 * nkv, TM_ATTN + ATTN_BLOCK), BF16),
                        pltpu.VMEM((TM_ATTN + ATTN_BLOCK, 4 * nkv), BF16), pltpu.VMEM((TM_ATTN, nq), BF16),
                        pltpu.VMEM((4 * nkv, ATTN_BLOCK), BF16), pltpu.VMEM((ATTN_BLOCK, 4 * nkv), BF16)],
        compiler_params=pltpu.CompilerParams(dimension_semantics=("arbitrary", "arbitrary"),
                                             vmem_limit_bytes=VMEM_LIMIT_BYTES),
        name="mixer1_swa",
    )(sinks.astype(F32), x2d, w_p, b_p, w_o.astype(BF16), _row(b_o), _row(ln_g), _row(ln_b), ones_blk)


def _ffn(x2d, layer, p3d, w_gate, w_up, w_down, ln_g, ln_b, w_ple, w_ple_gate, b_ple_gate):
    t = x2d.shape[0]
    tok = lambda width: pl.BlockSpec((TM_FFN, width), lambda i: (i, 0))

    def layer_resident(rows, cols):
        return pl.BlockSpec((None, rows, cols), lambda i: (layer, 0, 0), pipeline_mode=pl.Buffered(1))

    return pl.pallas_call(
        _ffn_kernel,
        grid=(t // TM_FFN,),
        in_specs=[tok(D_MODEL), pl.BlockSpec((None, TM_FFN, PLE_DIM), lambda i: (layer, i, 0)),
                  layer_resident(D_MODEL, D_FF), layer_resident(D_MODEL, D_FF), layer_resident(D_FF, D_MODEL),
                  _resident((1, D_MODEL)), _resident((1, D_MODEL)),
                  layer_resident(D_MODEL, D_MODEL), _resident((1, D_MODEL)), layer_resident(PLE_DIM, D_MODEL)],
        out_specs=tok(D_MODEL),
        out_shape=jax.ShapeDtypeStruct((t, D_MODEL), F32),
        compiler_params=pltpu.CompilerParams(dimension_semantics=("arbitrary",),
                                             vmem_limit_bytes=VMEM_LIMIT_BYTES),
        name="ffn_ple",
    )(x2d, p3d, w_gate, w_up, w_down, _row(ln_g[layer]), _row(ln_b[layer]),
      w_ple_gate, _row(b_ple_gate[layer]), w_ple)


def kernel(x, p, w_in_mix, ssd_conv_w, ssd_conv_b, ssd_dt_bias, ssd_a_log, ssd_d, ssd_norm_w, sc_conv_w, w_out_mix, w_qkv, b_qkv, attn_sinks, w_o, b_o, ln_mix_g, ln_mix_b, w_ffn_gate, w_ffn_up, w_ffn_down, ln_ffn_g, ln_ffn_b, w_ple, w_ple_gate, b_ple_gate):
    batch, seq, d = x.shape
    assert d == D_MODEL and seq % TM_MIX0 == 0 and seq % TM_ATTN == 0 and (batch * seq) % TM_FFN == 0
    assert p.shape == (DEPTH, batch, seq, PLE_DIM)
    h = x.reshape(batch * seq, d)
    p2d = p.reshape(DEPTH, batch * seq, PLE_DIM)

    ffn_w = [w.astype(BF16) for w in (w_ffn_gate, w_ffn_up, w_ffn_down)]
    ple_w = w_ple.astype(BF16)
    ple_gate_w = w_ple_gate.astype(BF16)

    def ffn(h, i):
        return _ffn(h, i, p2d, *ffn_w, ln_ffn_g, ln_ffn_b, ple_w, ple_gate_w, b_ple_gate)

    h = _mixer0(h, batch, w_in_mix[0], ssd_conv_w[0], ssd_conv_b[0], ssd_dt_bias[0], ssd_a_log[0], ssd_d[0],
                ssd_norm_w[0], sc_conv_w[0], w_out_mix[0], ln_mix_g[0], ln_mix_b[0])
    h = ffn(h, 0)
    h = _attn(h, batch, w_qkv[0], b_qkv[0], attn_sinks[0], w_o[0], b_o[0], ln_mix_g[1], ln_mix_b[1])
    h = ffn(h, 1)
    return h.reshape(batch, seq, d)
```

```python
import functools

import jax
import jax.numpy as jnp
from jax import lax
from jax.experimental import pallas as pl
from jax.experimental.pallas import tpu as pltpu

F32 = jnp.float32
BF16 = jnp.bfloat16

D_MODEL = 1024
DEPTH = 2
PLE_DIM = 256
SSD_HEADS = 16
SSD_HEAD_DIM = 64
SSD_INNER = SSD_HEADS * SSD_HEAD_DIM
SSD_GROUPS = 2
SSD_HPG = SSD_HEADS // SSD_GROUPS
SSD_STATE = 128
SSD_CONV = 4
SSD_CHUNK = 128
SSD_XBC = SSD_INNER + 2 * SSD_GROUPS * SSD_STATE
SC_DIM = 1024
SC_CONV = 3
MIX_OUT = SSD_INNER + SC_DIM
ATTN_HEADS = 16
ATTN_KV_HEADS = 4
ATTN_HEAD_DIM = 64
ATTN_BLOCK = 128
D_FF = ((8 * D_MODEL + 3 * 256 - 1) // (3 * 256)) * 256
ALPHA = (2 * DEPTH) ** 0.25
LN_EPS = 1e-5
RMS_EPS = 1e-5

LANES = 128
SUBLANES = 8
V7X_VMEM_BYTES = 64 * 1024 * 1024
VMEM_LIMIT_BYTES = V7X_VMEM_BYTES - 8 * 1024 * 1024

TM_MIX0 = 512
TM_ATTN = 1024
TM_FFN = 1024
FFN_ROWS = 256

_C_Z = 0
_C_XBC = _C_Z + SSD_INNER
_C_SCB = _C_XBC + SSD_XBC
_C_SCC = _C_SCB + SC_DIM
_C_SCH = _C_SCC + SC_DIM
_C_DT = _C_SCH + SC_DIM
MIX_IN_P = _C_DT + LANES
DT_REP = 2
COL_CHUNK = 256


def _dot(a, b):
    return jnp.dot(a, b, preferred_element_type=F32)


def _dot_nt(a, b):
    return lax.dot_general(a, b, (((1,), (1,)), ((), ())), preferred_element_type=F32)


def _dot_tn(a, b):
    return lax.dot_general(a, b, (((0,), (0,)), ((), ())), preferred_element_type=F32)


def _sigmoid(x):
    return 1.0 / (1.0 + jnp.exp(-x))


def _silu(x):
    return x * _sigmoid(x)


def _layer_norm(r, g, b):
    mu = jnp.mean(r, axis=-1, keepdims=True)
    xc = r - mu
    var = jnp.mean(xc * xc, axis=-1, keepdims=True)
    return xc * lax.rsqrt(var + LN_EPS) * g + b


def _split_bf16(v, parts):
    out = []
    rem = v
    for i in range(parts):
        p = rem.astype(BF16)
        out.append(p)
        if i + 1 < parts:
            rem = rem - p.astype(F32)
    return out


def _causal_conv(cur, prev_tail, w):
    width, cols = w.shape
    rows = lax.broadcasted_iota(jnp.int32, (SUBLANES, cols), 0)
    acc = cur * w[width - 1:width, :]
    for s in range(1, width):
        rolled = pltpu.roll(cur, s, 0)
        head = jnp.where(rows < s, pltpu.roll(prev_tail, s, 0), rolled[0:SUBLANES])
        shifted = jnp.concatenate([head, rolled[SUBLANES:]], axis=0)
        acc = acc + shifted * w[width - 1 - s:width - s, :]
    return acc


def _mixer0_kernel(x_ref, w_in_ref, cw_ref, cb_ref, dtb_ref, alog_ref, dexp_ref, nw_ref, scw_ref, w_out_ref,
                   g_ref, b_ref, e2_ref, tril_ref, o_ref,
                   xb_s, z_s, xbc_s, dt_s, acs_s, ycat_s, prev_xbc_s, prev_sc_s, h_s):
    tm = x_ref.shape[0]
    c = SSD_CHUNK
    gw = SSD_HPG * SSD_HEAD_DIM

    @pl.when(pl.program_id(1) == 0)
    def _():
        prev_xbc_s[...] = jnp.zeros_like(prev_xbc_s)
        prev_sc_s[...] = jnp.zeros_like(prev_sc_s)
        h_s[...] = jnp.zeros_like(h_s)

    xb_s[...] = x_ref[...].astype(BF16)

    def proj(lo, width):
        return _dot(xb_s[...], w_in_ref[:, lo:lo + width])

    def xbc_piece(k):
        cols = slice(k * COL_CHUNK, (k + 1) * COL_CHUNK)
        st = {}

        def mm():
            st["raw"] = proj(_C_XBC + k * COL_CHUNK, COL_CHUNK)

        def vpu():
            raw = st.pop("raw")
            xbc_s[:, cols] = _silu(_causal_conv(raw, prev_xbc_s[:, cols], cw_ref[:, cols]) + cb_ref[:, cols])
            prev_xbc_s[:, cols] = raw[tm - SUBLANES:tm]

        return mm, vpu

    def dt_piece():
        st = {}

        def mm():
            st["raw"] = proj(_C_DT, LANES)

        def vpu():
            pre = st.pop("raw") + dtb_ref[...]
            dt = jnp.maximum(pre, 0.0) + jnp.log1p(jnp.exp(-jnp.abs(pre)))
            dt_s[...] = dt
            lane_row = lax.broadcasted_iota(jnp.int32, (1, LANES), 1)
            da = dt * jnp.where(lane_row < DT_REP * SSD_HEADS, -jnp.exp(alog_ref[...]), 0.0)
            da = jnp.concatenate([da[i * c:(i + 1) * c] for i in range(tm // c)], axis=1)
            acs_s[...] = _dot(tril_ref[...], jnp.concatenate(_split_bf16(da, 3), axis=0))

        return mm, vpu

    def z_piece(k):
        def mm():
            z_s[:, k * COL_CHUNK:(k + 1) * COL_CHUNK] = proj(_C_Z + k * COL_CHUNK, COL_CHUNK)

        return mm

    def sc_piece(k):
        cols = slice(k * COL_CHUNK, (k + 1) * COL_CHUNK)
        st = {}

        def mm_b():
            st["b"] = proj(_C_SCB + k * COL_CHUNK, COL_CHUNK)

        def mm_c():
            st["c"] = proj(_C_SCC + k * COL_CHUNK, COL_CHUNK)

        def mm_h():
            st["u"] = st.pop("c") * proj(_C_SCH + k * COL_CHUNK, COL_CHUNK)

        def vpu():
            u = st.pop("u")
            y_sc = st.pop("b") * _causal_conv(u, prev_sc_s[:, cols], scw_ref[:, cols])
            ycat_s[:, SSD_INNER + k * COL_CHUNK:SSD_INNER + (k + 1) * COL_CHUNK] = y_sc.astype(BF16)
            prev_sc_s[:, cols] = u[tm - SUBLANES:tm]

        return mm_b, mm_c, mm_h, vpu

    def out_piece(i):
        rows = slice(i * c, (i + 1) * c)
        st = {}

        def sc_half():
            st["mix"] = _dot(ycat_s[rows, SSD_INNER:], w_out_ref[SSD_INNER:, :])

        def ssd_half():
            mix = st.pop("mix") + _dot(ycat_s[rows, :SSD_INNER], w_out_ref[:SSD_INNER, :])
            o_ref[rows, :] = _layer_norm(ALPHA * x_ref[rows, :] + mix, g_ref[...], b_ref[...])

        return sc_half, ssd_half

    def ssd_chunk(ci):
        rows = slice(ci * c, (ci + 1) * c)
        st = {}

        def s1():
            lane_sq = lax.broadcasted_iota(jnp.int32, (c, LANES), 1)
            dt = dt_s[rows, :]
            a_cs = acs_s[:, ci * LANES:(ci + 1) * LANES]
            ecs = jnp.exp(a_cs)
            dtdte = dt * jnp.exp(a_cs[c - 1:c, :] - a_cs)
            q = jnp.where(lane_sq < SSD_HEADS, dtdte, ecs)
            ex = _dot(jnp.concatenate(_split_bf16(q, 2), axis=1), e2_ref[...])
            st["dtdte_e"] = ex[:, :SSD_INNER]
            st["ecs_e"] = ex[:, SSD_INNER:]
            st["a_cs"] = a_cs
            st["t"] = jnp.where(lane_sq < SSD_HEADS, a_cs, dt).T

        def s2():
            group_lane = lax.broadcasted_iota(jnp.int32, (c, SSD_GROUPS * SSD_STATE), 1) // SSD_STATE
            bmf = xbc_s[rows, SSD_INNER:SSD_INNER + SSD_GROUPS * SSD_STATE]
            bm = bmf.astype(BF16)
            cm = xbc_s[rows, SSD_INNER + SSD_GROUPS * SSD_STATE:].astype(BF16)
            st["bm"] = bm
            st["cm"] = cm
            b_diag = jnp.concatenate([jnp.where(group_lane == g, bmf, 0.0) for g in range(SSD_GROUPS)], axis=0)
            st["cb"] = _dot_nt(cm, b_diag.astype(BF16))
            st["y"] = []

        def s2_quad(qi):
            lane_sq = lax.broadcasted_iota(jnp.int32, (c, LANES), 1)
            causal = lax.broadcasted_iota(jnp.int32, (c, LANES), 0) >= lane_sq
            quad_lane = lax.broadcasted_iota(jnp.int32, (c, 4 * SSD_HEAD_DIM), 1) // SSD_HEAD_DIM
            g = (4 * qi) // SSD_HPG
            h0 = 4 * qi
            a_cs = st["a_cs"]
            t = st["t"]
            cb = st["cb"][:, g * c:(g + 1) * c]
            ms = []
            for h in range(h0, h0 + 4):
                seg = jnp.broadcast_to(a_cs[:, h:h + 1], (c, LANES)) - t[h:h + 1, :]
                dec = jnp.exp(jnp.where(causal, seg, -jnp.inf))
                ms.append((cb * dec * t[SSD_HEADS + h:SSD_HEADS + h + 1, :]).astype(BF16))
            xq = xbc_s[rows, h0 * SSD_HEAD_DIM:(h0 + 4) * SSD_HEAD_DIM]
            rhs = jnp.concatenate([jnp.where(quad_lane == k, xq, 0.0) for k in range(4)], axis=0)
            st["y"].append(_dot(jnp.concatenate(ms, axis=1), rhs.astype(BF16)))

        def s3():
            for key in ("a_cs", "t", "cb"):
                st.pop(key)
            bm = st.pop("bm")
            cm = st.pop("cm")
            ecs_e = st["ecs_e"]
            dtdte_e = st.pop("dtdte_e")
            y_off = []
            for g in range(SSD_GROUPS):
                hg = h_s[g]
                y_off.append(_dot(cm[:, g * SSD_STATE:(g + 1) * SSD_STATE], hg.astype(BF16)))
                xd = (xbc_s[rows, g * gw:(g + 1) * gw] * dtdte_e[:, g * gw:(g + 1) * gw]).astype(BF16)
                new = _dot_tn(bm[:, g * SSD_STATE:(g + 1) * SSD_STATE], xd)
                h_s[g] = hg * ecs_e[c - 1:c, g * gw:(g + 1) * gw] + new
            st["y_off"] = jnp.concatenate(y_off, axis=1)

        def s4():
            y = jnp.concatenate(st.pop("y"), axis=1) + st.pop("y_off") * st.pop("ecs_e")
            y = y + xbc_s[rows, :SSD_INNER] * dexp_ref[...]
            y = y * _silu(z_s[rows, :])
            yn = []
            for g in range(SSD_GROUPS):
                yg = y[:, g * gw:(g + 1) * gw]
                yn.append(yg * lax.rsqrt(jnp.mean(yg * yg, axis=-1, keepdims=True) + RMS_EPS))
            ycat_s[rows, 0:SSD_INNER] = (jnp.concatenate(yn, axis=1) * nw_ref[...]).astype(BF16)

        quads = [functools.partial(s2_quad, qi) for qi in range(SSD_HEADS // 4)]
        return [s1, s2] + quads + [s3, s4]

    head = [xbc_piece(k) for k in range(SSD_XBC // COL_CHUNK)] + [dt_piece()]
    z_tasks = [z_piece(k) for k in range(SSD_INNER // COL_CHUNK)]
    pending = None
    for mm, vpu in head:
        mm()
        if pending is not None:
            if z_tasks:
                z_tasks.pop(0)()
            pending()
        pending = vpu
    pending()

    n_chunks = tm // c
    fillers = [(0, t) for t in z_tasks]
    for k in range(SC_DIM // COL_CHUNK):
        fillers.extend((0, t) for t in sc_piece(k))
    outs = [out_piece(i) for i in range(n_chunks)]
    fillers.extend((0, sc_half) for sc_half, _ in outs)
    per_chunk = len(ssd_chunk(0))
    fillers.extend((per_chunk * (i + 1), ssd_half) for i, (_, ssd_half) in enumerate(outs))
    stages = [s for ci in range(n_chunks) for s in ssd_chunk(ci)]
    n_fill = len(fillers)
    for i, stage in enumerate(stages):
        want = ((i + 1) * n_fill) // len(stages) - (n_fill - len(fillers))
        while want > 0 and fillers and fillers[0][0] <= i:
            fillers.pop(0)[1]()
            want -= 1
        stage()
    for _, task in fillers:
        task()


def _attn_kernel(sink_ref, x_ref, w_qkv_ref, b_qkv_ref, w_o_ref, b_o_ref, g_ref, b_ref, ones_ref, o_ref,
                 xb_s, q_s, kt_s, v_s, a_s, kprev_s, vprev_s):
    tm = x_ref.shape[0]
    blk = ATTN_BLOCK
    first_tile = pl.program_id(1) == 0
    nq = ATTN_HEADS * ATTN_HEAD_DIM
    nkv = ATTN_KV_HEADS * ATTN_HEAD_DIM
    nkv2 = ATTN_KV_HEADS * LANES
    half = ATTN_HEAD_DIM

    @pl.when(first_tile)
    def _():
        kprev_s[...] = jnp.zeros_like(kprev_s)
        vprev_s[...] = jnp.zeros_like(vprev_s)

    kt_s[:, :blk] = kprev_s[...]
    v_s[:blk, :] = vprev_s[...]
    xb_s[...] = x_ref[...].astype(BF16)

    def proj_piece(k):
        lo = k * COL_CHUNK
        st = {}

        def mm():
            st["v"] = _dot(xb_s[...], w_qkv_ref[:, lo:lo + COL_CHUNK]) + b_qkv_ref[:, lo:lo + COL_CHUNK]

        def q_vpu():
            q_s[:, lo:lo + COL_CHUNK] = (st.pop("v") * (ATTN_HEAD_DIM ** -0.5)).astype(BF16)

        def k_vpu():
            kt = st.pop("v").T.astype(BF16)
            zero = jnp.zeros((half, tm), BF16)
            for h in range(ATTN_KV_HEADS):
                kh = kt[h * half:(h + 1) * half]
                for base, part in ((0, jnp.concatenate([kh, zero], axis=0)),
                                   (nkv2, jnp.concatenate([zero, kh], axis=0))):
                    kt_s[base + h * LANES:base + (h + 1) * LANES, blk:] = part
                    kprev_s[base + h * LANES:base + (h + 1) * LANES, :] = part[:, tm - blk:]

        def v_vpu():
            vd = st.pop("v")
            lo_c = lax.broadcasted_iota(jnp.int32, (tm, LANES), 1) < half
            for c2 in range(ATTN_KV_HEADS // 2):
                col = vd[:, c2 * LANES:(c2 + 1) * LANES]
                swapped = pltpu.roll(col, half, 1)
                planes = ((0, 2 * c2, jnp.where(lo_c, col, 0.0)), (nkv2, 2 * c2, jnp.where(lo_c, 0.0, swapped)),
                          (0, 2 * c2 + 1, jnp.where(lo_c, swapped, 0.0)), (nkv2, 2 * c2 + 1, jnp.where(lo_c, 0.0, col)))
                for base, h, part in planes:
                    part = part.astype(BF16)
                    v_s[blk:, base + h * LANES:base + (h + 1) * LANES] = part
                    vprev_s[:, base + h * LANES:base + (h + 1) * LANES] = part[tm - blk:]

        vpu = q_vpu if lo < nq else (k_vpu if lo < nq + nkv else v_vpu)
        return mm, vpu

    pending = None
    for k in range((nq + 2 * nkv) // COL_CHUNK):
        mm, vpu = proj_piece(k)
        mm()
        if pending is not None:
            pending()
        pending = vpu
    pending()

    qi = lax.broadcasted_iota(jnp.int32, (blk, 2 * blk), 0)
    kj = lax.broadcasted_iota(jnp.int32, (blk, 2 * blk), 1)
    band = (kj > qi) & (kj <= qi + blk)
    band0 = band & (kj >= jnp.where(first_tile, blk, 0))
    lane_o = lax.broadcasted_iota(jnp.int32, (blk, LANES), 1) < half

    def scores(n, j):
        c0 = (j // 2) * LANES
        keys = slice(n * blk, (n + 2) * blk)
        kk = jnp.concatenate([kt_s[c0:c0 + LANES, keys], kt_s[nkv2 + c0:nkv2 + c0 + LANES, keys]], axis=1)
        return _dot(q_s[n * blk:(n + 1) * blk, j * LANES:(j + 1) * LANES], kk)

    def probs(n, j, s):
        valid = band0 if n == 0 else band
        ps = []
        es = []
        for t in range(2):
            sink = sink_ref[2 * j + t]
            sh = jnp.where(valid, s[:, t * 2 * blk:(t + 1) * 2 * blk], -jnp.inf)
            m = jnp.maximum(jnp.max(sh, axis=-1, keepdims=True), sink)
            ps.append(jnp.exp(sh - m).astype(BF16))
            es.append(jnp.exp(sink - m))
        return jnp.concatenate(ps, axis=1), jnp.where(lane_o, es[0], es[1])

    def output(n, j, p, es):
        c0 = (j // 2) * LANES
        keys = slice(n * blk, (n + 2) * blk)
        vv = jnp.concatenate([v_s[keys, c0:c0 + LANES], v_s[keys, nkv2 + c0:nkv2 + c0 + LANES]], axis=0)
        r = _dot(p, jnp.concatenate([vv, ones_ref[...]], axis=1))
        a_s[n * blk:(n + 1) * blk, j * LANES:(j + 1) * LANES] = (r[:, :LANES] / (r[:, LANES:] + es)).astype(BF16)

    def out_block(n):
        rows = slice(n * blk, (n + 1) * blk)
        st = {}

        def mm():
            st["mix"] = _dot(a_s[rows, :], w_o_ref[...]) + b_o_ref[...]

        def norm():
            o_ref[rows, :] = _layer_norm(ALPHA * x_ref[rows, :] + st.pop("mix"), g_ref[...], b_ref[...])

        return mm, norm

    steps = [(n, j) for n in range(tm // blk) for j in range(ATTN_HEADS // 2)]
    pairs = ATTN_HEADS // 2
    extra = {}
    for n in range(tm // blk - 1):
        mm, norm = out_block(n)
        extra[pairs * (n + 1) + 3] = mm
        extra[pairs * (n + 1) + 6] = norm
    s_val = {}
    p_val = {}
    for i in range(len(steps) + 2):
        if i < len(steps):
            s_val[i] = scores(*steps[i])
        if i in extra:
            extra[i]()
        if 0 <= i - 1 < len(steps):
            p_val[i - 1] = probs(*steps[i - 1], s_val.pop(i - 1))
        if 0 <= i - 2 < len(steps):
            output(*steps[i - 2], *p_val.pop(i - 2))
    for task in out_block(tm // blk - 1):
        task()


def _ffn_kernel(x_ref, p_ref, wg_ref, wu_ref, wd_ref, g_ref, b_ref, wpg_ref, bpg_ref, wple_ref, o_ref):
    tm = x_ref.shape[0]
    n_blk = tm // FFN_ROWS

    def blk(i):
        rows = slice(i * FFN_ROWS, (i + 1) * FFN_ROWS)
        st = {}

        def gate_up():
            xb = x_ref[rows, :].astype(BF16)
            st["g"] = _dot(xb, wg_ref[...])
            st["u"] = _dot(xb, wu_ref[...])

        def act():
            st["h"] = (_silu(st.pop("g")) * st.pop("u")).astype(BF16)

        def down():
            st["y"] = _dot(st.pop("h"), wd_ref[...])

        def norm():
            st["x2"] = _layer_norm(ALPHA * x_ref[rows, :] + st.pop("y"), g_ref[...], b_ref[...])

        def ple():
            st["gate"] = _dot(st["x2"].astype(BF16), wpg_ref[...])
            st["pe"] = _dot(p_ref[rows, :].astype(BF16), wple_ref[...])

        def out():
            o_ref[rows, :] = st.pop("x2") + st.pop("pe") * _sigmoid(st.pop("gate") + bpg_ref[...])

        return [gate_up, act, down, norm, ple, out]

    stages = [blk(i) for i in range(n_blk)]
    n_st = len(stages[0])
    for step in range(0, n_st + 2 * (n_blk - 1), 2):
        for i in range(n_blk):
            k = step - 2 * i
            if 0 <= k < n_st:
                stages[i][k]()
        for i in range(n_blk):
            k = step - 2 * i + 1
            if 0 <= k < n_st:
                stages[i][k]()


def _resident(shape):
    return pl.BlockSpec(shape, lambda *_: (0,) * len(shape), pipeline_mode=pl.Buffered(1))


def _row(v):
    return v.reshape(1, -1).astype(F32)


def _pad_lanes(v, width):
    return jnp.pad(v, ((0, 0), (0, width - v.shape[1])))


def _mixer0(x2d, batch, w_in, conv_w, conv_b, dt_bias, a_log, d_skip, norm_w, sc_conv_w, w_out, ln_g, ln_b):
    t = x2d.shape[0]
    n_l = t // batch // TM_MIX0
    o1 = SSD_INNER
    o2 = o1 + SSD_XBC
    o3 = o2 + SSD_HEADS
    w_in_p = jnp.concatenate(
        [w_in[:, :o2], w_in[:, o3:], _pad_lanes(jnp.tile(w_in[:, o2:o3], (1, DT_REP)), LANES)], axis=1).astype(BF16)
    dtb = _pad_lanes(jnp.tile(_row(dt_bias), (1, DT_REP)), LANES)
    alog = _pad_lanes(jnp.tile(_row(a_log), (1, DT_REP)), LANES)
    dexp = jnp.repeat(_row(d_skip), SSD_HEAD_DIM, axis=1)
    r = jnp.arange(LANES)[:, None]
    col = jnp.arange(DT_REP * SSD_INNER)[None, :]
    e1 = ((r < DT_REP * SSD_HEADS) & (col // SSD_HEAD_DIM == r)).astype(BF16)
    e2 = jnp.concatenate([e1, e1], axis=0)
    tri = (jnp.arange(SSD_CHUNK)[:, None] >= jnp.arange(SSD_CHUNK)[None, :]).astype(BF16)
    tril3 = jnp.concatenate([tri, tri, tri], axis=1)

    tok = lambda width: pl.BlockSpec((TM_MIX0, width), lambda b, l: (b * n_l + l, 0))
    return pl.pallas_call(
        _mixer0_kernel,
        grid=(batch, n_l),
        in_specs=[tok(D_MODEL), _resident((D_MODEL, MIX_IN_P)), _resident((SSD_CONV, SSD_XBC)),
                  _resident((1, SSD_XBC)), _resident((1, LANES)), _resident((1, LANES)),
                  _resident((1, SSD_INNER)), _resident((1, SSD_INNER)), _resident((SC_CONV, SC_DIM)),
                  _resident((MIX_OUT, D_MODEL)), _resident((1, D_MODEL)), _resident((1, D_MODEL)),
                  _resident(e2.shape), _resident(tril3.shape)],
        out_specs=tok(D_MODEL),
        out_shape=jax.ShapeDtypeStruct((t, D_MODEL), F32),
        scratch_shapes=[pltpu.VMEM((TM_MIX0, D_MODEL), BF16),
                        pltpu.VMEM((TM_MIX0, SSD_INNER), F32), pltpu.VMEM((TM_MIX0, SSD_XBC), F32),
                        pltpu.VMEM((TM_MIX0, LANES), F32), pltpu.VMEM((SSD_CHUNK, TM_MIX0), F32),
                        pltpu.VMEM((TM_MIX0, MIX_OUT), BF16),
                        pltpu.VMEM((SUBLANES, SSD_XBC), F32), pltpu.VMEM((SUBLANES, SC_DIM), F32),
                        pltpu.VMEM((SSD_GROUPS, SSD_STATE, SSD_HPG * SSD_HEAD_DIM), F32)],
        compiler_params=pltpu.CompilerParams(dimension_semantics=("arbitrary", "arbitrary"),
                                             vmem_limit_bytes=VMEM_LIMIT_BYTES),
        name="mixer0_ssd_shortconv",
    )(x2d, w_in_p, conv_w.astype(F32), _row(conv_b), dtb, alog, dexp, _row(norm_w), sc_conv_w.astype(F32),
      w_out.astype(BF16), _row(ln_g), _row(ln_b), e2, tril3)


def _attn(x2d, batch, w_qkv, b_qkv, sinks, w_o, b_o, ln_g, ln_b):
    t = x2d.shape[0]
    n_l = t // batch // TM_ATTN
    nq = ATTN_HEADS * ATTN_HEAD_DIM
    nkv = ATTN_KV_HEADS * ATTN_HEAD_DIM

    width = nq + 2 * nkv
    ones_blk = (jnp.arange(4 * ATTN_BLOCK)[:, None] // (2 * ATTN_BLOCK) == jnp.arange(LANES)[None, :] // ATTN_HEAD_DIM
                ).astype(BF16)
    tok = pl.BlockSpec((TM_ATTN, D_MODEL), lambda b, l: (b * n_l + l, 0))
    return pl.pallas_call(
        _attn_kernel,
        grid=(batch, n_l),
        in_specs=[pl.BlockSpec(memory_space=pltpu.SMEM), tok, _resident((D_MODEL, width)), _resident((1, width)),
                  _resident((nq, D_MODEL)), _resident((1, D_MODEL)), _resident((1, D_MODEL)),
                  _resident((1, D_MODEL)), _resident(ones_blk.shape)],
        out_specs=tok,
        out_shape=jax.ShapeDtypeStruct((t, D_MODEL), F32),
        scratch_shapes=[pltpu.VMEM((TM_ATTN, D_MODEL), BF16),
                        pltpu.VMEM((TM_ATTN, nq), BF16), pltpu.VMEM((4 * nkv, TM_ATTN + ATTN_BLOCK), BF16),
                        pltpu.VMEM((TM_ATTN + ATTN_BLOCK, 4 * nkv), BF16), pltpu.VMEM((TM_ATTN, nq), BF16),
                        pltpu.VMEM((4 * nkv, ATTN_BLOCK), BF16), pltpu.VMEM((ATTN_BLOCK, 4 * nkv), BF16)],
        compiler_params=pltpu.CompilerParams(dimension_semantics=("arbitrary", "arbitrary"),
                                             vmem_limit_bytes=VMEM_LIMIT_BYTES),
        name="mixer1_swa",
    )(sinks.astype(F32), x2d, w_qkv.astype(BF16), _row(b_qkv), w_o.astype(BF16), _row(b_o), _row(ln_g), _row(ln_b),
      ones_blk)


def _ffn(x2d, layer, p3d, w_gate, w_up, w_down, ln_g, ln_b, w_ple, w_ple_gate, b_ple_gate):
    t = x2d.shape[0]
    tok = lambda width: pl.BlockSpec((TM_FFN, width), lambda i: (i, 0))

    def layer_resident(rows, cols):
        return pl.BlockSpec((None, rows, cols), lambda i: (layer, 0, 0), pipeline_mode=pl.Buffered(1))

    return pl.pallas_call(
        _ffn_kernel,
        grid=(t // TM_FFN,),
        in_specs=[tok(D_MODEL), pl.BlockSpec((None, TM_FFN, PLE_DIM), lambda i: (layer, i, 0)),
                  layer_resident(D_MODEL, D_FF), layer_resident(D_MODEL, D_FF), layer_resident(D_FF, D_MODEL),
                  _resident((1, D_MODEL)), _resident((1, D_MODEL)),
                  layer_resident(D_MODEL, D_MODEL), _resident((1, D_MODEL)), layer_resident(PLE_DIM, D_MODEL)],
        out_specs=tok(D_MODEL),
        out_shape=jax.ShapeDtypeStruct((t, D_MODEL), F32),
        compiler_params=pltpu.CompilerParams(dimension_semantics=("arbitrary",),
                                             vmem_limit_bytes=VMEM_LIMIT_BYTES),
        name="ffn_ple",
    )(x2d, p3d, w_gate, w_up, w_down, _row(ln_g[layer]), _row(ln_b[layer]),
      w_ple_gate, _row(b_ple_gate[layer]), w_ple)


def kernel(x, p, w_in_mix, ssd_conv_w, ssd_conv_b, ssd_dt_bias, ssd_a_log, ssd_d, ssd_norm_w, sc_conv_w, w_out_mix, w_qkv, b_qkv, attn_sinks, w_o, b_o, ln_mix_g, ln_mix_b, w_ffn_gate, w_ffn_up, w_ffn_down, ln_ffn_g, ln_ffn_b, w_ple, w_ple_gate, b_ple_gate):
    batch, seq, d = x.shape
    assert d == D_MODEL and seq % TM_MIX0 == 0 and seq % TM_ATTN == 0 and (batch * seq) % TM_FFN == 0
    assert p.shape == (DEPTH, batch, seq, PLE_DIM)
    h = x.reshape(batch * seq, d)
    p2d = p.reshape(DEPTH, batch * seq, PLE_DIM)

    ffn_w = [w.astype(BF16) for w in (w_ffn_gate, w_ffn_up, w_ffn_down)]
    ple_w = w_ple.astype(BF16)
    ple_gate_w = w_ple_gate.astype(BF16)

    def ffn(h, i):
        return _ffn(h, i, p2d, *ffn_w, ln_ffn_g, ln_ffn_b, ple_w, ple_gate_w, b_ple_gate)

    h = _mixer0(h, batch, w_in_mix[0], ssd_conv_w[0], ssd_conv_b[0], ssd_dt_bias[0], ssd_a_log[0], ssd_d[0],
                ssd_norm_w[0], sc_conv_w[0], w_out_mix[0], ln_mix_g[0], ln_mix_b[0])
    h = ffn(h, 0)
    h = _attn(h, batch, w_qkv[0], b_qkv[0], attn_sinks[0], w_o[0], b_o[0], ln_mix_g[1], ln_mix_b[1])
    h = ffn(h, 1)
    return h.reshape(batch, seq, d)
```

```python
import functools

import jax
import jax.numpy as jnp
from jax import lax
from jax.experimental import pallas as pl
from jax.experimental.pallas import tpu as pltpu

F32 = jnp.float32
BF16 = jnp.bfloat16

D_MODEL = 1024
DEPTH = 2
PLE_DIM = 256
SSD_HEADS = 16
SSD_HEAD_DIM = 64
SSD_INNER = SSD_HEADS * SSD_HEAD_DIM
SSD_GROUPS = 2
SSD_HPG = SSD_HEADS // SSD_GROUPS
SSD_STATE = 128
SSD_CONV = 4
SSD_CHUNK = 128
SSD_XBC = SSD_INNER + 2 * SSD_GROUPS * SSD_STATE
SC_DIM = 1024
SC_CONV = 3
MIX_OUT = SSD_INNER + SC_DIM
ATTN_HEADS = 16
ATTN_KV_HEADS = 4
ATTN_HEAD_DIM = 64
ATTN_BLOCK = 128
D_FF = ((8 * D_MODEL + 3 * 256 - 1) // (3 * 256)) * 256
ALPHA = (2 * DEPTH) ** 0.25
LN_EPS = 1e-5
RMS_EPS = 1e-5

LANES = 128
SUBLANES = 8
BF16_SUBLANES = 16
V7X_VMEM_BYTES = 64 * 1024 * 1024
VMEM_LIMIT_BYTES = V7X_VMEM_BYTES - 8 * 1024 * 1024

TM_MIX0 = 512
TM_ATTN = 1024
TM_FFN = 1024
FFN_ROWS = 256

_C_Z = 0
_C_XBC = _C_Z + SSD_INNER
_C_SCB = 0
_C_SCC = _C_SCB + SC_DIM
_C_SCH = _C_SCC + SC_DIM
DT_REP = 2
COL_CHUNK = 256


def _dot(a, b):
    return jnp.dot(a, b, preferred_element_type=F32)


def _dot_nt(a, b):
    return lax.dot_general(a, b, (((1,), (1,)), ((), ())), preferred_element_type=F32)


def _dot_tn(a, b):
    return lax.dot_general(a, b, (((0,), (0,)), ((), ())), preferred_element_type=F32)


def _sigmoid(x):
    return 1.0 / (1.0 + jnp.exp(-x))


def _silu(x):
    return x * _sigmoid(x)


def _layer_norm(r, g, b):
    mu = jnp.mean(r, axis=-1, keepdims=True)
    xc = r - mu
    var = jnp.mean(xc * xc, axis=-1, keepdims=True)
    return xc * lax.rsqrt(var + LN_EPS) * g + b


def _split_bf16(v, parts):
    out = []
    rem = v
    for i in range(parts):
        p = rem.astype(BF16)
        out.append(p)
        if i + 1 < parts:
            rem = rem - p.astype(F32)
    return out


def _causal_conv(cur, prev_tail, w):
    width, cols = w.shape
    rows = lax.broadcasted_iota(jnp.int32, (SUBLANES, cols), 0)
    acc = cur * w[width - 1:width, :]
    for s in range(1, width):
        rolled = pltpu.roll(cur, s, 0)
        head = jnp.where(rows < s, pltpu.roll(prev_tail, s, 0), rolled[0:SUBLANES])
        shifted = jnp.concatenate([head, rolled[SUBLANES:]], axis=0)
        acc = acc + shifted * w[width - 1 - s:width - s, :]
    return acc


def _mixer0_kernel(x_ref, w_zx_ref, w_sc_ref, w_dt_ref, cw_ref, cb_ref, dtb_ref, alog_ref, dexp_ref, nw_ref, scw_ref, w_out_ref,
                   g_ref, b_ref, e2_ref, tril_ref, o_ref,
                   xb_s, z_s, xbc_s, dt_s, acs_s, ycat_s, prev_xbc_s, prev_sc_s, h_s):
    tm = x_ref.shape[0]
    c = SSD_CHUNK
    gw = SSD_HPG * SSD_HEAD_DIM

    @pl.when(pl.program_id(1) == 0)
    def _():
        prev_xbc_s[...] = jnp.zeros_like(prev_xbc_s)
        prev_sc_s[...] = jnp.zeros_like(prev_sc_s)
        h_s[...] = jnp.zeros_like(h_s)

    xb_s[...] = x_ref[...].astype(BF16)

    def proj(w_ref, lo, width):
        return _dot(xb_s[...], w_ref[:, lo:lo + width])

    def xbc_piece(k):
        cols = slice(k * COL_CHUNK, (k + 1) * COL_CHUNK)
        st = {}

        def mm():
            st["raw"] = proj(w_zx_ref, _C_XBC + k * COL_CHUNK, COL_CHUNK)

        def vpu():
            raw = st.pop("raw")
            xbc_s[:, cols] = _silu(_causal_conv(raw, prev_xbc_s[:, cols], cw_ref[:, cols]) + cb_ref[:, cols])
            prev_xbc_s[:, cols] = raw[tm - SUBLANES:tm]

        return mm, vpu

    def dt_piece():
        st = {}

        def mm():
            st["raw"] = proj(w_dt_ref, 0, LANES)

        def vpu():
            pre = st.pop("raw") + dtb_ref[...]
            dt = jnp.maximum(pre, 0.0) + jnp.log1p(jnp.exp(-jnp.abs(pre)))
            dt_s[...] = dt
            lane_row = lax.broadcasted_iota(jnp.int32, (1, LANES), 1)
            da = dt * jnp.where(lane_row < DT_REP * SSD_HEADS, -jnp.exp(alog_ref[...]), 0.0)
            da = jnp.concatenate([da[i * c:(i + 1) * c] for i in range(tm // c)], axis=1)
            acs_s[...] = _dot(tril_ref[...], jnp.concatenate(_split_bf16(da, 3), axis=0))

        return mm, vpu

    def z_piece(k):
        def mm():
            z_s[:, k * COL_CHUNK:(k + 1) * COL_CHUNK] = proj(w_zx_ref, _C_Z + k * COL_CHUNK, COL_CHUNK)

        return mm

    def sc_piece(k):
        cols = slice(k * COL_CHUNK, (k + 1) * COL_CHUNK)
        st = {}

        def mm_b():
            st["b"] = proj(w_sc_ref, _C_SCB + k * COL_CHUNK, COL_CHUNK)

        def mm_c():
            st["c"] = proj(w_sc_ref, _C_SCC + k * COL_CHUNK, COL_CHUNK)

        def mm_h():
            st["u"] = st.pop("c") * proj(w_sc_ref, _C_SCH + k * COL_CHUNK, COL_CHUNK)

        def vpu():
            u = st.pop("u")
            y_sc = st.pop("b") * _causal_conv(u, prev_sc_s[:, cols], scw_ref[:, cols])
            ycat_s[:, SSD_INNER + k * COL_CHUNK:SSD_INNER + (k + 1) * COL_CHUNK] = y_sc.astype(BF16)
            prev_sc_s[:, cols] = u[tm - SUBLANES:tm]

        return mm_b, mm_c, mm_h, vpu

    def out_piece(i):
        rows = slice(i * c, (i + 1) * c)
        st = {}

        def sc_half():
            st["mix"] = _dot(ycat_s[rows, SSD_INNER:], w_out_ref[SSD_INNER:, :])

        def ssd_half():
            mix = st.pop("mix") + _dot(ycat_s[rows, :SSD_INNER], w_out_ref[:SSD_INNER, :])
            o_ref[rows, :] = _layer_norm(ALPHA * x_ref[rows, :] + mix, g_ref[...], b_ref[...])

        return sc_half, ssd_half

    def ssd_chunk(ci):
        rows = slice(ci * c, (ci + 1) * c)
        st = {}

        def s1():
            lane_sq = lax.broadcasted_iota(jnp.int32, (c, LANES), 1)
            dt = dt_s[rows, :]
            a_cs = acs_s[:, ci * LANES:(ci + 1) * LANES]
            ecs = jnp.exp(a_cs)
            dtdte = dt * jnp.exp(a_cs[c - 1:c, :] - a_cs)
            q = jnp.where(lane_sq < SSD_HEADS, dtdte, ecs)
            ex = _dot(jnp.concatenate(_split_bf16(q, 2), axis=1), e2_ref[...])
            st["dtdte_e"] = ex[:, :SSD_INNER]
            st["ecs_e"] = ex[:, SSD_INNER:]
            st["a_cs"] = a_cs
            st["t"] = jnp.where(lane_sq < SSD_HEADS, a_cs, dt).T

        def s2():
            group_lane = lax.broadcasted_iota(jnp.int32, (c, SSD_GROUPS * SSD_STATE), 1) // SSD_STATE
            bmf = xbc_s[rows, SSD_INNER:SSD_INNER + SSD_GROUPS * SSD_STATE]
            bm = bmf.astype(BF16)
            cm = xbc_s[rows, SSD_INNER + SSD_GROUPS * SSD_STATE:].astype(BF16)
            st["bm"] = bm
            st["cm"] = cm
            b_diag = jnp.concatenate([jnp.where(group_lane == g, bmf, 0.0) for g in range(SSD_GROUPS)], axis=0)
            st["cb"] = _dot_nt(cm, b_diag.astype(BF16))
            st["y"] = []

        def s2_quad(qi):
            lane_sq = lax.broadcasted_iota(jnp.int32, (c, LANES), 1)
            causal = lax.broadcasted_iota(jnp.int32, (c, LANES), 0) >= lane_sq
            quad_lane = lax.broadcasted_iota(jnp.int32, (c, 4 * SSD_HEAD_DIM), 1) // SSD_HEAD_DIM
            g = (4 * qi) // SSD_HPG
            h0 = 4 * qi
            a_cs = st["a_cs"]
            t = st["t"]
            cb = st["cb"][:, g * c:(g + 1) * c]
            ms = []
            for h in range(h0, h0 + 4):
                seg = jnp.broadcast_to(a_cs[:, h:h + 1], (c, LANES)) - t[h:h + 1, :]
                dec = jnp.exp(jnp.where(causal, seg, -jnp.inf))
                ms.append((cb * dec * t[SSD_HEADS + h:SSD_HEADS + h + 1, :]).astype(BF16))
            xq = xbc_s[rows, h0 * SSD_HEAD_DIM:(h0 + 4) * SSD_HEAD_DIM]
            rhs = jnp.concatenate([jnp.where(quad_lane == k, xq, 0.0) for k in range(4)], axis=0)
            st["y"].append(_dot(jnp.concatenate(ms, axis=1), rhs.astype(BF16)))

        def s3():
            for key in ("a_cs", "t", "cb"):
                st.pop(key)
            bm = st.pop("bm")
            cm = st.pop("cm")
            ecs_e = st["ecs_e"]
            dtdte_e = st.pop("dtdte_e")
            y_off = []
            for g in range(SSD_GROUPS):
                hg = h_s[g]
                y_off.append(_dot(cm[:, g * SSD_STATE:(g + 1) * SSD_STATE], hg.astype(BF16)))
                xd = (xbc_s[rows, g * gw:(g + 1) * gw] * dtdte_e[:, g * gw:(g + 1) * gw]).astype(BF16)
                new = _dot_tn(bm[:, g * SSD_STATE:(g + 1) * SSD_STATE], xd)
                h_s[g] = hg * ecs_e[c - 1:c, g * gw:(g + 1) * gw] + new
            st["y_off"] = jnp.concatenate(y_off, axis=1)

        def s4():
            y = jnp.concatenate(st.pop("y"), axis=1) + st.pop("y_off") * st.pop("ecs_e")
            y = y + xbc_s[rows, :SSD_INNER] * dexp_ref[...]
            y = y * _silu(z_s[rows, :])
            yn = []
            for g in range(SSD_GROUPS):
                yg = y[:, g * gw:(g + 1) * gw]
                yn.append(yg * lax.rsqrt(jnp.mean(yg * yg, axis=-1, keepdims=True) + RMS_EPS))
            ycat_s[rows, 0:SSD_INNER] = (jnp.concatenate(yn, axis=1) * nw_ref[...]).astype(BF16)

        quads = [functools.partial(s2_quad, qi) for qi in range(SSD_HEADS // 4)]
        return [s1, s2] + quads + [s3, s4]

    head = [xbc_piece(k) for k in range(SSD_XBC // COL_CHUNK)] + [dt_piece()]
    z_tasks = [z_piece(k) for k in range(SSD_INNER // COL_CHUNK)]
    pending = None
    for mm, vpu in head:
        mm()
        if pending is not None:
            if z_tasks:
                z_tasks.pop(0)()
            pending()
        pending = vpu
    pending()

    n_chunks = tm // c
    fillers = [(0, t) for t in z_tasks]
    for k in range(SC_DIM // COL_CHUNK):
        fillers.extend((0, t) for t in sc_piece(k))
    outs = [out_piece(i) for i in range(n_chunks)]
    fillers.extend((0, sc_half) for sc_half, _ in outs)
    per_chunk = len(ssd_chunk(0))
    fillers.extend((per_chunk * (i + 1), ssd_half) for i, (_, ssd_half) in enumerate(outs))
    stages = [s for ci in range(n_chunks) for s in ssd_chunk(ci)]
    n_fill = len(fillers)
    for i, stage in enumerate(stages):
        want = ((i + 1) * n_fill) // len(stages) - (n_fill - len(fillers))
        while want > 0 and fillers and fillers[0][0] <= i:
            fillers.pop(0)[1]()
            want -= 1
        stage()
    for _, task in fillers:
        task()


def _attn_kernel(sink_ref, x_ref, w_qkv_ref, b_qkv_ref, w_o_ref, b_o_ref, g_ref, b_ref, ones_ref, o_ref,
                 xb_s, q_s, kt_s, v_s, a_s, kprev_s, vprev_s):
    tm = x_ref.shape[0]
    blk = ATTN_BLOCK
    first_tile = pl.program_id(1) == 0
    nq = ATTN_HEADS * ATTN_HEAD_DIM
    nkv = ATTN_KV_HEADS * ATTN_HEAD_DIM
    nkv2 = ATTN_KV_HEADS * LANES
    half = ATTN_HEAD_DIM

    @pl.when(first_tile)
    def _():
        kprev_s[...] = jnp.zeros_like(kprev_s)
        vprev_s[...] = jnp.zeros_like(vprev_s)

    kt_s[:, :blk] = kprev_s[...]
    v_s[:blk, :] = vprev_s[...]
    xb_s[...] = x_ref[...].astype(BF16)

    def proj_piece(k):
        lo = k * COL_CHUNK
        st = {}

        def mm():
            st["v"] = _dot(xb_s[...], w_qkv_ref[:, lo:lo + COL_CHUNK]) + b_qkv_ref[:, lo:lo + COL_CHUNK]

        def q_vpu():
            q_s[:, lo:lo + COL_CHUNK] = (st.pop("v") * (ATTN_HEAD_DIM ** -0.5)).astype(BF16)

        def k_vpu():
            kt = st.pop("v").T.astype(BF16)
            zero = jnp.zeros((half, tm), BF16)
            for h in range(ATTN_KV_HEADS):
                kh = kt[h * half:(h + 1) * half]
                for base, part in ((0, jnp.concatenate([kh, zero], axis=0)),
                                   (nkv2, jnp.concatenate([zero, kh], axis=0))):
                    kt_s[base + h * LANES:base + (h + 1) * LANES, blk:] = part
                    kprev_s[base + h * LANES:base + (h + 1) * LANES, :] = part[:, tm - blk:]

        def v_vpu():
            vd = st.pop("v")
            lo_c = lax.broadcasted_iota(jnp.int32, (tm, LANES), 1) < half
            for c2 in range(ATTN_KV_HEADS // 2):
                col = vd[:, c2 * LANES:(c2 + 1) * LANES]
                swapped = pltpu.roll(col, half, 1)
                planes = ((0, 2 * c2, jnp.where(lo_c, col, 0.0)), (nkv2, 2 * c2, jnp.where(lo_c, 0.0, swapped)),
                          (0, 2 * c2 + 1, jnp.where(lo_c, swapped, 0.0)), (nkv2, 2 * c2 + 1, jnp.where(lo_c, 0.0, col)))
                for base, h, part in planes:
                    part = part.astype(BF16)
                    v_s[blk:, base + h * LANES:base + (h + 1) * LANES] = part
                    vprev_s[:, base + h * LANES:base + (h + 1) * LANES] = part[tm - blk:]

        vpu = q_vpu if lo < nq else (k_vpu if lo < nq + nkv else v_vpu)
        return mm, vpu

    pending = None
    for k in range((nq + 2 * nkv) // COL_CHUNK):
        mm, vpu = proj_piece(k)
        mm()
        if pending is not None:
            pending()
        pending = vpu
    pending()

    qi = lax.broadcasted_iota(jnp.int32, (blk, 2 * blk), 0)
    kj = lax.broadcasted_iota(jnp.int32, (blk, 2 * blk), 1)
    band = (kj > qi) & (kj <= qi + blk)
    band0 = band & (kj >= jnp.where(first_tile, blk, 0))
    lane_o = lax.broadcasted_iota(jnp.int32, (blk, LANES), 1) < half

    def scores(n, j):
        c0 = (j // 2) * LANES
        keys = slice(n * blk, (n + 2) * blk)
        kk = jnp.concatenate([kt_s[c0:c0 + LANES, keys], kt_s[nkv2 + c0:nkv2 + c0 + LANES, keys]], axis=1)
        return _dot(q_s[n * blk:(n + 1) * blk, j * LANES:(j + 1) * LANES], kk)

    def probs(n, j, s):
        valid = band0 if n == 0 else band
        ps = []
        es = []
        for t in range(2):
            sink = sink_ref[2 * j + t]
            sh = jnp.where(valid, s[:, t * 2 * blk:(t + 1) * 2 * blk], -jnp.inf)
            m = jnp.maximum(jnp.max(sh, axis=-1, keepdims=True), sink)
            ps.append(jnp.exp(sh - m).astype(BF16))
            es.append(jnp.exp(sink - m))
        return jnp.concatenate(ps, axis=1), jnp.where(lane_o, es[0], es[1])

    def output(n, j, p, es):
        c0 = (j // 2) * LANES
        keys = slice(n * blk, (n + 2) * blk)
        vv = jnp.concatenate([v_s[keys, c0:c0 + LANES], v_s[keys, nkv2 + c0:nkv2 + c0 + LANES]], axis=0)
        r = _dot(p, jnp.concatenate([vv, ones_ref[...]], axis=1))
        a_s[n * blk:(n + 1) * blk, j * LANES:(j + 1) * LANES] = (r[:, :LANES] / (r[:, LANES:] + es)).astype(BF16)

    def out_block(n):
        rows = slice(n * blk, (n + 1) * blk)
        st = {}

        def mm():
            st["mix"] = _dot(a_s[rows, :], w_o_ref[...]) + b_o_ref[...]

        def norm():
            o_ref[rows, :] = _layer_norm(ALPHA * x_ref[rows, :] + st.pop("mix"), g_ref[...], b_ref[...])

        return mm, norm

    steps = [(n, j) for n in range(tm // blk) for j in range(ATTN_HEADS // 2)]
    pairs = ATTN_HEADS // 2
    extra = {}
    for n in range(tm // blk - 1):
        mm, norm = out_block(n)
        extra[pairs * (n + 1) + 3] = mm
        extra[pairs * (n + 1) + 6] = norm
    s_val = {}
    p_val = {}
    for i in range(len(steps) + 2):
        if i < len(steps):
            s_val[i] = scores(*steps[i])
        if i in extra:
            extra[i]()
        if 0 <= i - 1 < len(steps):
            p_val[i - 1] = probs(*steps[i - 1], s_val.pop(i - 1))
        if 0 <= i - 2 < len(steps):
            output(*steps[i - 2], *p_val.pop(i - 2))
    for task in out_block(tm // blk - 1):
        task()


def _ffn_kernel(x_ref, p_ref, wg_ref, wu_ref, wd_ref, g_ref, b_ref, wpg_ref, bpg_ref, wple_ref, o_ref):
    tm = x_ref.shape[0]
    n_blk = tm // FFN_ROWS

    def blk(i):
        rows = slice(i * FFN_ROWS, (i + 1) * FFN_ROWS)
        st = {}

        def gate_up():
            xb = x_ref[rows, :].astype(BF16)
            st["g"] = _dot(xb, wg_ref[...])
            st["u"] = _dot(xb, wu_ref[...])

        def act():
            st["h"] = (_silu(st.pop("g")) * st.pop("u")).astype(BF16)

        def down():
            st["y"] = _dot(st.pop("h"), wd_ref[...])

        def norm():
            st["x2"] = _layer_norm(ALPHA * x_ref[rows, :] + st.pop("y"), g_ref[...], b_ref[...])

        def ple():
            st["gate"] = _dot(st["x2"].astype(BF16), wpg_ref[...])
            st["pe"] = _dot(p_ref[rows, :].astype(BF16), wple_ref[...])

        def out():
            o_ref[rows, :] = st.pop("x2") + st.pop("pe") * _sigmoid(st.pop("gate") + bpg_ref[...])

        return [gate_up, act, down, norm, ple, out]

    stages = [blk(i) for i in range(n_blk)]
    n_st = len(stages[0])
    for step in range(0, n_st + 2 * (n_blk - 1), 2):
        for i in range(n_blk):
            k = step - 2 * i
            if 0 <= k < n_st:
                stages[i][k]()
        for i in range(n_blk):
            k = step - 2 * i + 1
            if 0 <= k < n_st:
                stages[i][k]()


def _with_casts(body, n_in, n_cast):
    def kernel_fn(*refs):
        ins, casts_in = refs[:n_in], refs[n_in:n_in + n_cast]
        out, casts_out = refs[n_in + n_cast], refs[n_in + n_cast + 1:n_in + 2 * n_cast + 1]
        scratch = refs[n_in + 2 * n_cast + 1:]
        body(*ins, out, *scratch)
        for src, dst in zip(casts_in, casts_out):
            dst[...] = src[...].astype(BF16)

    return kernel_fn


def _cast_plan(w, layer, steps, flat_step):
    _, rows, cols = w.shape
    per, reuse = rows // steps, 1
    while per % BF16_SUBLANES:
        per, reuse = per * 2, reuse * 2
    assert rows % per == 0
    in_spec = pl.BlockSpec((None, per, cols), lambda *g: (layer, flat_step(*g) // reuse, 0))
    out_spec = pl.BlockSpec((per, cols), lambda *g: (flat_step(*g) // reuse, 0))
    return in_spec, out_spec, jax.ShapeDtypeStruct((rows, cols), BF16)


def _resident(shape):
    return pl.BlockSpec(shape, lambda *_: (0,) * len(shape), pipeline_mode=pl.Buffered(1))


def _row(v):
    return v.reshape(1, -1).astype(F32)


def _pad_lanes(v, width):
    return jnp.pad(v, ((0, 0), (0, width - v.shape[1])))


def _mixer0(x2d, batch, w_in, conv_w, conv_b, dt_bias, a_log, d_skip, norm_w, sc_conv_w, w_out, ln_g, ln_b, next_w):
    t = x2d.shape[0]
    n_l = t // batch // TM_MIX0
    o1 = SSD_INNER
    o2 = o1 + SSD_XBC
    o3 = o2 + SSD_HEADS
    w_zx = w_in[:, :o2].astype(BF16)
    w_sc = w_in[:, o3:].astype(BF16)
    w_dt = _pad_lanes(jnp.tile(w_in[:, o2:o3], (1, DT_REP)), LANES).astype(BF16)
    dtb = _pad_lanes(jnp.tile(_row(dt_bias), (1, DT_REP)), LANES)
    alog = _pad_lanes(jnp.tile(_row(a_log), (1, DT_REP)), LANES)
    dexp = jnp.repeat(_row(d_skip), SSD_HEAD_DIM, axis=1)
    r = jnp.arange(LANES)[:, None]
    col = jnp.arange(DT_REP * SSD_INNER)[None, :]
    e1 = ((r < DT_REP * SSD_HEADS) & (col // SSD_HEAD_DIM == r)).astype(BF16)
    e2 = jnp.concatenate([e1, e1], axis=0)
    tri = (jnp.arange(SSD_CHUNK)[:, None] >= jnp.arange(SSD_CHUNK)[None, :]).astype(BF16)
    tril3 = jnp.concatenate([tri, tri, tri], axis=1)

    tok = lambda width: pl.BlockSpec((TM_MIX0, width), lambda b, l: (b * n_l + l, 0))
    plans = [_cast_plan(w, layer, batch * n_l, lambda b, l: b * n_l + l) for w, layer in next_w]
    in_specs = [tok(D_MODEL), _resident(w_zx.shape), _resident(w_sc.shape), _resident(w_dt.shape),
                _resident((SSD_CONV, SSD_XBC)),
                _resident((1, SSD_XBC)), _resident((1, LANES)), _resident((1, LANES)),
                _resident((1, SSD_INNER)), _resident((1, SSD_INNER)), _resident((SC_CONV, SC_DIM)),
                _resident((MIX_OUT, D_MODEL)), _resident((1, D_MODEL)), _resident((1, D_MODEL)),
                _resident(e2.shape), _resident(tril3.shape)]
    outs = pl.pallas_call(
        _with_casts(_mixer0_kernel, len(in_specs), len(plans)),
        grid=(batch, n_l),
        in_specs=in_specs + [p[0] for p in plans],
        out_specs=[tok(D_MODEL)] + [p[1] for p in plans],
        out_shape=[jax.ShapeDtypeStruct((t, D_MODEL), F32)] + [p[2] for p in plans],
        scratch_shapes=[pltpu.VMEM((TM_MIX0, D_MODEL), BF16),
                        pltpu.VMEM((TM_MIX0, SSD_INNER), F32), pltpu.VMEM((TM_MIX0, SSD_XBC), F32),
                        pltpu.VMEM((TM_MIX0, LANES), F32), pltpu.VMEM((SSD_CHUNK, TM_MIX0), F32),
                        pltpu.VMEM((TM_MIX0, MIX_OUT), BF16),
                        pltpu.VMEM((SUBLANES, SSD_XBC), F32), pltpu.VMEM((SUBLANES, SC_DIM), F32),
                        pltpu.VMEM((SSD_GROUPS, SSD_STATE, SSD_HPG * SSD_HEAD_DIM), F32)],
        compiler_params=pltpu.CompilerParams(dimension_semantics=("arbitrary", "arbitrary"),
                                             vmem_limit_bytes=VMEM_LIMIT_BYTES),
        name="mixer0_ssd_shortconv",
    )(x2d, w_zx, w_sc, w_dt, conv_w.astype(F32), _row(conv_b), dtb, alog, dexp, _row(norm_w), sc_conv_w.astype(F32),
      w_out.astype(BF16), _row(ln_g), _row(ln_b), e2, tril3, *[w for w, _ in next_w])
    return outs[0], outs[1:]


def _attn(x2d, batch, w_qkv, b_qkv, sinks, w_o, b_o, ln_g, ln_b, next_w):
    t = x2d.shape[0]
    n_l = t // batch // TM_ATTN
    nq = ATTN_HEADS * ATTN_HEAD_DIM
    nkv = ATTN_KV_HEADS * ATTN_HEAD_DIM

    width = nq + 2 * nkv
    ones_blk = (jnp.arange(4 * ATTN_BLOCK)[:, None] // (2 * ATTN_BLOCK) == jnp.arange(LANES)[None, :] // ATTN_HEAD_DIM
                ).astype(BF16)
    tok = pl.BlockSpec((TM_ATTN, D_MODEL), lambda b, l: (b * n_l + l, 0))
    plans = [_cast_plan(w, layer, batch * n_l, lambda b, l: b * n_l + l) for w, layer in next_w]
    in_specs = [pl.BlockSpec(memory_space=pltpu.SMEM), tok, _resident((D_MODEL, width)), _resident((1, width)),
                _resident((nq, D_MODEL)), _resident((1, D_MODEL)), _resident((1, D_MODEL)),
                _resident((1, D_MODEL)), _resident(ones_blk.shape)]
    outs = pl.pallas_call(
        _with_casts(_attn_kernel, len(in_specs), len(plans)),
        grid=(batch, n_l),
        in_specs=in_specs + [p[0] for p in plans],
        out_specs=[tok] + [p[1] for p in plans],
        out_shape=[jax.ShapeDtypeStruct((t, D_MODEL), F32)] + [p[2] for p in plans],
        scratch_shapes=[pltpu.VMEM((TM_ATTN, D_MODEL), BF16),
                        pltpu.VMEM((TM_ATTN, nq), BF16), pltpu.VMEM((4 * nkv, TM_ATTN + ATTN_BLOCK), BF16),
                        pltpu.VMEM((TM_ATTN + ATTN_BLOCK, 4 * nkv), BF16), pltpu.VMEM((TM_ATTN, nq), BF16),
                        pltpu.VMEM((4 * nkv, ATTN_BLOCK), BF16), pltpu.VMEM((ATTN_BLOCK, 4 * nkv), BF16)],
        compiler_params=pltpu.CompilerParams(dimension_semantics=("arbitrary", "arbitrary"),
                                             vmem_limit_bytes=VMEM_LIMIT_BYTES),
        name="mixer1_swa",
    )(sinks.astype(F32), x2d, w_qkv, _row(b_qkv), w_o, _row(b_o), _row(ln_g), _row(ln_b), ones_blk,
      *[w for w, _ in next_w])
    return outs[0], outs[1:]


def _ffn(x2d, layer, p3d, w_gate, w_up, w_down, ln_g, ln_b, w_ple_gate, b_ple_gate, w_ple, next_w):
    t = x2d.shape[0]
    steps = t // TM_FFN
    tok = lambda width: pl.BlockSpec((TM_FFN, width), lambda i: (i, 0))
    plans = [_cast_plan(w, lyr, steps, lambda i: i) for w, lyr in next_w]
    in_specs = [tok(D_MODEL), pl.BlockSpec((None, TM_FFN, PLE_DIM), lambda i: (layer, i, 0)),
                _resident((D_MODEL, D_FF)), _resident((D_MODEL, D_FF)), _resident((D_FF, D_MODEL)),
                _resident((1, D_MODEL)), _resident((1, D_MODEL)),
                _resident((D_MODEL, D_MODEL)), _resident((1, D_MODEL)), _resident((PLE_DIM, D_MODEL))]
    outs = pl.pallas_call(
        _with_casts(_ffn_kernel, len(in_specs), len(plans)),
        grid=(steps,),
        in_specs=in_specs + [p[0] for p in plans],
        out_specs=[tok(D_MODEL)] + [p[1] for p in plans],
        out_shape=[jax.ShapeDtypeStruct((t, D_MODEL), F32)] + [p[2] for p in plans],
        compiler_params=pltpu.CompilerParams(dimension_semantics=("arbitrary",),
                                             vmem_limit_bytes=VMEM_LIMIT_BYTES),
        name="ffn_ple",
    )(x2d, p3d, w_gate, w_up, w_down, _row(ln_g[layer]), _row(ln_b[layer]),
      w_ple_gate, _row(b_ple_gate[layer]), w_ple, *[w for w, _ in next_w])
    return outs[0], outs[1:]


def kernel(x, p, w_in_mix, ssd_conv_w, ssd_conv_b, ssd_dt_bias, ssd_a_log, ssd_d, ssd_norm_w, sc_conv_w, w_out_mix, w_qkv, b_qkv, attn_sinks, w_o, b_o, ln_mix_g, ln_mix_b, w_ffn_gate, w_ffn_up, w_ffn_down, ln_ffn_g, ln_ffn_b, w_ple, w_ple_gate, b_ple_gate):
    batch, seq, d = x.shape
    assert d == D_MODEL and seq % TM_MIX0 == 0 and seq % TM_ATTN == 0 and (batch * seq) % TM_FFN == 0
    assert p.shape == (DEPTH, batch, seq, PLE_DIM)
    h = x.reshape(batch * seq, d)
    p2d = p.reshape(DEPTH, batch * seq, PLE_DIM)

    def ffn_weights(i):
        return [(w, i) for w in (w_ffn_gate, w_ffn_up, w_ffn_down, w_ple_gate, w_ple)]

    h, ffn0_w = _mixer0(h, batch, w_in_mix[0], ssd_conv_w[0], ssd_conv_b[0], ssd_dt_bias[0], ssd_a_log[0], ssd_d[0],
                        ssd_norm_w[0], sc_conv_w[0], w_out_mix[0], ln_mix_g[0], ln_mix_b[0], ffn_weights(0))
    h, attn_w = _ffn(h, 0, p2d, *ffn0_w[:3], ln_ffn_g, ln_ffn_b, ffn0_w[3], b_ple_gate, ffn0_w[4],
                     [(w_qkv, 0), (w_o, 0)])
    h, ffn1_w = _attn(h, batch, attn_w[0], b_qkv[0], attn_sinks[0], attn_w[1], b_o[0], ln_mix_g[1], ln_mix_b[1],
                      ffn_weights(1))
    h, _ = _ffn(h, 1, p2d, *ffn1_w[:3], ln_ffn_g, ln_ffn_b, ffn1_w[3], b_ple_gate, ffn1_w[4], [])
    return h.reshape(batch, seq, d)
```

```python
import functools

import jax
import jax.numpy as jnp
from jax import lax
from jax.experimental import pallas as pl
from jax.experimental.pallas import tpu as pltpu

F32 = jnp.float32
BF16 = jnp.bfloat16

D_MODEL = 1024
DEPTH = 2
PLE_DIM = 256
SSD_HEADS = 16
SSD_HEAD_DIM = 64
SSD_INNER = SSD_HEADS * SSD_HEAD_DIM
SSD_GROUPS = 2
SSD_HPG = SSD_HEADS // SSD_GROUPS
SSD_STATE = 128
SSD_CONV = 4
SSD_CHUNK = 128
SSD_XBC = SSD_INNER + 2 * SSD_GROUPS * SSD_STATE
SC_DIM = 1024
SC_CONV = 3
MIX_OUT = SSD_INNER + SC_DIM
ATTN_HEADS = 16
ATTN_KV_HEADS = 4
ATTN_HEAD_DIM = 64
ATTN_BLOCK = 128
D_FF = ((8 * D_MODEL + 3 * 256 - 1) // (3 * 256)) * 256
ALPHA = (2 * DEPTH) ** 0.25
LN_EPS = 1e-5
LOG2E = 1.4426950408889634
RMS_EPS = 1e-5

LANES = 128
SUBLANES = 8
BF16_SUBLANES = 16
V7X_VMEM_BYTES = 64 * 1024 * 1024
VMEM_LIMIT_BYTES = V7X_VMEM_BYTES - 8 * 1024 * 1024

TM_MIX0 = 512
TM_ATTN = 1024
TM_FFN = 1024
FFN_ROWS = 256

_C_Z = 0
_C_XBC = _C_Z + SSD_INNER
_C_SCB = 0
_C_SCC = _C_SCB + SC_DIM
_C_SCH = _C_SCC + SC_DIM
DT_REP = 2
COL_CHUNK = 256


def _dot(a, b):
    return jnp.dot(a, b, preferred_element_type=F32)


def _dot_nt(a, b):
    return lax.dot_general(a, b, (((1,), (1,)), ((), ())), preferred_element_type=F32)


def _dot_tn(a, b):
    return lax.dot_general(a, b, (((0,), (0,)), ((), ())), preferred_element_type=F32)


def _sigmoid(x):
    return 1.0 / (1.0 + jnp.exp(-x))


def _silu(x):
    return x * _sigmoid(x)


def _layer_norm(r, g, b):
    mu = jnp.mean(r, axis=-1, keepdims=True)
    xc = r - mu
    var = jnp.mean(xc * xc, axis=-1, keepdims=True)
    return xc * lax.rsqrt(var + LN_EPS) * g + b


def _split_bf16(v, parts):
    out = []
    rem = v
    for i in range(parts):
        p = rem.astype(BF16)
        out.append(p)
        if i + 1 < parts:
            rem = rem - p.astype(F32)
    return out


def _causal_conv(cur, prev_tail, w):
    width, cols = w.shape
    rows = lax.broadcasted_iota(jnp.int32, (SUBLANES, cols), 0)
    acc = cur * w[width - 1:width, :]
    for s in range(1, width):
        rolled = pltpu.roll(cur, s, 0)
        head = jnp.where(rows < s, pltpu.roll(prev_tail, s, 0), rolled[0:SUBLANES])
        shifted = jnp.concatenate([head, rolled[SUBLANES:]], axis=0)
        acc = acc + shifted * w[width - 1 - s:width - s, :]
    return acc


def _mixer0_kernel(x_ref, w_zx_ref, w_sc_ref, w_dt_ref, cw_ref, cb_ref, dtb_ref, alog_ref, dexp_ref, nw_ref, scw_ref, w_out_ref,
                   g_ref, b_ref, e2_ref, tril_ref, o_ref,
                   xb_s, z_s, xbc_s, dt_s, acs_s, ycat_s, prev_xbc_s, prev_sc_s, h_s):
    tm = x_ref.shape[0]
    c = SSD_CHUNK
    gw = SSD_HPG * SSD_HEAD_DIM

    @pl.when(pl.program_id(1) == 0)
    def _():
        prev_xbc_s[...] = jnp.zeros_like(prev_xbc_s)
        prev_sc_s[...] = jnp.zeros_like(prev_sc_s)
        h_s[...] = jnp.zeros_like(h_s)

    xb_s[...] = x_ref[...].astype(BF16)

    def proj(w_ref, lo, width):
        return _dot(xb_s[...], w_ref[:, lo:lo + width])

    def xbc_piece(k):
        cols = slice(k * COL_CHUNK, (k + 1) * COL_CHUNK)
        st = {}

        def mm():
            st["raw"] = proj(w_zx_ref, _C_XBC + k * COL_CHUNK, COL_CHUNK)

        def vpu():
            raw = st.pop("raw")
            xbc_s[:, cols] = _silu(_causal_conv(raw, prev_xbc_s[:, cols], cw_ref[:, cols]) + cb_ref[:, cols])
            prev_xbc_s[:, cols] = raw[tm - SUBLANES:tm]

        return mm, vpu

    def dt_piece():
        st = {}

        def mm():
            st["raw"] = proj(w_dt_ref, 0, LANES)

        def vpu():
            pre = st.pop("raw") + dtb_ref[...]
            dt = jnp.maximum(pre, 0.0) + jnp.log1p(jnp.exp(-jnp.abs(pre)))
            dt_s[...] = dt
            lane_row = lax.broadcasted_iota(jnp.int32, (1, LANES), 1)
            da = dt * jnp.where(lane_row < DT_REP * SSD_HEADS, -jnp.exp(alog_ref[...]), 0.0)
            da = jnp.concatenate([da[i * c:(i + 1) * c] for i in range(tm // c)], axis=1)
            acs_s[...] = _dot(tril_ref[...], jnp.concatenate(_split_bf16(da, 3), axis=0))

        return mm, vpu

    def z_piece(k):
        def mm():
            z_s[:, k * COL_CHUNK:(k + 1) * COL_CHUNK] = proj(w_zx_ref, _C_Z + k * COL_CHUNK, COL_CHUNK)

        return mm

    def sc_piece(k):
        cols = slice(k * COL_CHUNK, (k + 1) * COL_CHUNK)
        st = {}

        def mm_b():
            st["b"] = proj(w_sc_ref, _C_SCB + k * COL_CHUNK, COL_CHUNK)

        def mm_c():
            st["c"] = proj(w_sc_ref, _C_SCC + k * COL_CHUNK, COL_CHUNK)

        def mm_h():
            st["u"] = st.pop("c") * proj(w_sc_ref, _C_SCH + k * COL_CHUNK, COL_CHUNK)

        def vpu():
            u = st.pop("u")
            y_sc = st.pop("b") * _causal_conv(u, prev_sc_s[:, cols], scw_ref[:, cols])
            ycat_s[:, SSD_INNER + k * COL_CHUNK:SSD_INNER + (k + 1) * COL_CHUNK] = y_sc.astype(BF16)
            prev_sc_s[:, cols] = u[tm - SUBLANES:tm]

        return mm_b, mm_c, mm_h, vpu

    def out_piece(i):
        rows = slice(i * c, (i + 1) * c)
        st = {}

        def sc_half():
            st["mix"] = _dot(ycat_s[rows, SSD_INNER:], w_out_ref[SSD_INNER:, :])

        def ssd_half():
            mix = st.pop("mix") + _dot(ycat_s[rows, :SSD_INNER], w_out_ref[:SSD_INNER, :])
            o_ref[rows, :] = _layer_norm(ALPHA * x_ref[rows, :] + mix, g_ref[...], b_ref[...])

        return sc_half, ssd_half

    def ssd_chunk(ci):
        rows = slice(ci * c, (ci + 1) * c)
        st = {}

        def s1():
            lane_sq = lax.broadcasted_iota(jnp.int32, (c, LANES), 1)
            dt = dt_s[rows, :]
            a_cs = acs_s[:, ci * LANES:(ci + 1) * LANES]
            ecs = jnp.exp(a_cs)
            dtdte = dt * jnp.exp(a_cs[c - 1:c, :] - a_cs)
            q = jnp.where(lane_sq < SSD_HEADS, dtdte, ecs)
            ex = _dot(jnp.concatenate(_split_bf16(q, 2), axis=1), e2_ref[...])
            st["dtdte_e"] = ex[:, :SSD_INNER]
            st["ecs_e"] = ex[:, SSD_INNER:]
            st["a_cs"] = a_cs
            st["t"] = jnp.where(lane_sq < SSD_HEADS, a_cs, dt).T

        def s2():
            group_lane = lax.broadcasted_iota(jnp.int32, (c, SSD_GROUPS * SSD_STATE), 1) // SSD_STATE
            bmf = xbc_s[rows, SSD_INNER:SSD_INNER + SSD_GROUPS * SSD_STATE]
            bm = bmf.astype(BF16)
            cm = xbc_s[rows, SSD_INNER + SSD_GROUPS * SSD_STATE:].astype(BF16)
            st["bm"] = bm
            st["cm"] = cm
            b_diag = jnp.concatenate([jnp.where(group_lane == g, bmf, 0.0) for g in range(SSD_GROUPS)], axis=0)
            st["cb"] = _dot_nt(cm, b_diag.astype(BF16))
            st["y"] = []

        def s2_quad(qi):
            lane_sq = lax.broadcasted_iota(jnp.int32, (c, LANES), 1)
            causal = lax.broadcasted_iota(jnp.int32, (c, LANES), 0) >= lane_sq
            quad_lane = lax.broadcasted_iota(jnp.int32, (c, 4 * SSD_HEAD_DIM), 1) // SSD_HEAD_DIM
            g = (4 * qi) // SSD_HPG
            h0 = 4 * qi
            a_cs = st["a_cs"]
            t = st["t"]
            cb = st["cb"][:, g * c:(g + 1) * c]
            ms = []
            for h in range(h0, h0 + 4):
                seg = jnp.broadcast_to(a_cs[:, h:h + 1], (c, LANES)) - t[h:h + 1, :]
                dec = jnp.exp(jnp.where(causal, seg, -jnp.inf))
                ms.append((cb * dec * t[SSD_HEADS + h:SSD_HEADS + h + 1, :]).astype(BF16))
            xq = xbc_s[rows, h0 * SSD_HEAD_DIM:(h0 + 4) * SSD_HEAD_DIM]
            rhs = jnp.concatenate([jnp.where(quad_lane == k, xq, 0.0) for k in range(4)], axis=0)
            st["y"].append(_dot(jnp.concatenate(ms, axis=1), rhs.astype(BF16)))

        def s3():
            for key in ("a_cs", "t", "cb"):
                st.pop(key)
            bm = st.pop("bm")
            cm = st.pop("cm")
            ecs_e = st["ecs_e"]
            dtdte_e = st.pop("dtdte_e")
            y_off = []
            for g in range(SSD_GROUPS):
                hg = h_s[g]
                y_off.append(_dot(cm[:, g * SSD_STATE:(g + 1) * SSD_STATE], hg.astype(BF16)))
                xd = (xbc_s[rows, g * gw:(g + 1) * gw] * dtdte_e[:, g * gw:(g + 1) * gw]).astype(BF16)
                new = _dot_tn(bm[:, g * SSD_STATE:(g + 1) * SSD_STATE], xd)
                h_s[g] = hg * ecs_e[c - 1:c, g * gw:(g + 1) * gw] + new
            st["y_off"] = jnp.concatenate(y_off, axis=1)

        def s4():
            y = jnp.concatenate(st.pop("y"), axis=1) + st.pop("y_off") * st.pop("ecs_e")
            y = y + xbc_s[rows, :SSD_INNER] * dexp_ref[...]
            y = y * _silu(z_s[rows, :])
            yn = []
            for g in range(SSD_GROUPS):
                yg = y[:, g * gw:(g + 1) * gw]
                yn.append(yg * lax.rsqrt(jnp.mean(yg * yg, axis=-1, keepdims=True) + RMS_EPS))
            ycat_s[rows, 0:SSD_INNER] = (jnp.concatenate(yn, axis=1) * nw_ref[...]).astype(BF16)

        quads = [functools.partial(s2_quad, qi) for qi in range(SSD_HEADS // 4)]
        return [s1, s2] + quads + [s3, s4]

    head = [xbc_piece(k) for k in range(SSD_XBC // COL_CHUNK)] + [dt_piece()]
    z_tasks = [z_piece(k) for k in range(SSD_INNER // COL_CHUNK)]
    pending = None
    for mm, vpu in head:
        mm()
        if pending is not None:
            if z_tasks:
                z_tasks.pop(0)()
            pending()
        pending = vpu
    pending()

    n_chunks = tm // c
    fillers = [(0, t) for t in z_tasks]
    for k in range(SC_DIM // COL_CHUNK):
        fillers.extend((0, t) for t in sc_piece(k))
    outs = [out_piece(i) for i in range(n_chunks)]
    fillers.extend((0, sc_half) for sc_half, _ in outs)
    per_chunk = len(ssd_chunk(0))
    fillers.extend((per_chunk * (i + 1), ssd_half) for i, (_, ssd_half) in enumerate(outs))
    stages = [s for ci in range(n_chunks) for s in ssd_chunk(ci)]
    n_fill = len(fillers)
    for i, stage in enumerate(stages):
        want = ((i + 1) * n_fill) // len(stages) - (n_fill - len(fillers))
        while want > 0 and fillers and fillers[0][0] <= i:
            fillers.pop(0)[1]()
            want -= 1
        stage()
    for _, task in fillers:
        task()


def _attn_kernel(sink_ref, x_ref, w_qkv_ref, b_qkv_ref, w_o_ref, b_o_ref, g_ref, b_ref, ones_ref, o_ref,
                 xb_s, q_s, kt_s, v_s, a_s, kprev_s, vprev_s):
    tm = x_ref.shape[0]
    blk = ATTN_BLOCK
    first_tile = pl.program_id(1) == 0
    nq = ATTN_HEADS * ATTN_HEAD_DIM
    nkv = ATTN_KV_HEADS * ATTN_HEAD_DIM
    nkv2 = ATTN_KV_HEADS * LANES
    half = ATTN_HEAD_DIM

    @pl.when(first_tile)
    def _():
        kprev_s[...] = jnp.zeros_like(kprev_s)
        vprev_s[...] = jnp.zeros_like(vprev_s)

    kt_s[:, :blk] = kprev_s[...]
    v_s[:blk, :] = vprev_s[...]
    xb_s[...] = x_ref[...].astype(BF16)

    def proj_piece(k):
        lo = k * COL_CHUNK
        st = {}

        def mm():
            st["v"] = _dot(xb_s[...], w_qkv_ref[:, lo:lo + COL_CHUNK]) + b_qkv_ref[:, lo:lo + COL_CHUNK]

        def q_vpu():
            q_s[:, lo:lo + COL_CHUNK] = (st.pop("v") * (ATTN_HEAD_DIM ** -0.5 * LOG2E)).astype(BF16)

        def k_vpu():
            kt = st.pop("v").T.astype(BF16)
            zero = jnp.zeros((half, tm), BF16)
            for h in range(ATTN_KV_HEADS):
                kh = kt[h * half:(h + 1) * half]
                for base, part in ((0, jnp.concatenate([kh, zero], axis=0)),
                                   (nkv2, jnp.concatenate([zero, kh], axis=0))):
                    kt_s[base + h * LANES:base + (h + 1) * LANES, blk:] = part
                    kprev_s[base + h * LANES:base + (h + 1) * LANES, :] = part[:, tm - blk:]

        def v_vpu():
            vd = st.pop("v")
            lo_c = lax.broadcasted_iota(jnp.int32, (tm, LANES), 1) < half
            for c2 in range(ATTN_KV_HEADS // 2):
                col = vd[:, c2 * LANES:(c2 + 1) * LANES]
                swapped = pltpu.roll(col, half, 1)
                planes = ((0, 2 * c2, jnp.where(lo_c, col, 0.0)), (nkv2, 2 * c2, jnp.where(lo_c, 0.0, swapped)),
                          (0, 2 * c2 + 1, jnp.where(lo_c, swapped, 0.0)), (nkv2, 2 * c2 + 1, jnp.where(lo_c, 0.0, col)))
                for base, h, part in planes:
                    part = part.astype(BF16)
                    v_s[blk:, base + h * LANES:base + (h + 1) * LANES] = part
                    vprev_s[:, base + h * LANES:base + (h + 1) * LANES] = part[tm - blk:]

        vpu = q_vpu if lo < nq else (k_vpu if lo < nq + nkv else v_vpu)
        return mm, vpu

    pending = None
    for k in range((nq + 2 * nkv) // COL_CHUNK):
        mm, vpu = proj_piece(k)
        mm()
        if pending is not None:
            pending()
        pending = vpu
    pending()

    qi = lax.broadcasted_iota(jnp.int32, (blk, 2 * blk), 0)
    kj = lax.broadcasted_iota(jnp.int32, (blk, 2 * blk), 1)
    band = (kj > qi) & (kj <= qi + blk)
    band0 = band & (kj >= jnp.where(first_tile, blk, 0))
    lane_o = lax.broadcasted_iota(jnp.int32, (blk, LANES), 1) < half

    def scores(n, j):
        c0 = (j // 2) * LANES
        keys = slice(n * blk, (n + 2) * blk)
        kk = jnp.concatenate([kt_s[c0:c0 + LANES, keys], kt_s[nkv2 + c0:nkv2 + c0 + LANES, keys]], axis=1)
        return _dot(q_s[n * blk:(n + 1) * blk, j * LANES:(j + 1) * LANES], kk)

    def probs(n, j, s):
        valid = band0 if n == 0 else band
        ps = []
        es = []
        for t in range(2):
            sink = sink_ref[2 * j + t] * LOG2E
            sh = jnp.where(valid, s[:, t * 2 * blk:(t + 1) * 2 * blk], -jnp.inf)
            m = jnp.maximum(jnp.max(sh, axis=-1, keepdims=True), sink)
            ps.append(jnp.exp2(sh - m).astype(BF16))
            es.append(jnp.exp2(sink - m))
        return jnp.concatenate(ps, axis=1), jnp.where(lane_o, es[0], es[1])

    def output(n, j, p, es):
        c0 = (j // 2) * LANES
        keys = slice(n * blk, (n + 2) * blk)
        vv = jnp.concatenate([v_s[keys, c0:c0 + LANES], v_s[keys, nkv2 + c0:nkv2 + c0 + LANES]], axis=0)
        r = _dot(p, jnp.concatenate([vv, ones_ref[...]], axis=1))
        a_s[n * blk:(n + 1) * blk, j * LANES:(j + 1) * LANES] = (r[:, :LANES] / (r[:, LANES:] + es)).astype(BF16)

    def out_block(n):
        rows = slice(n * blk, (n + 1) * blk)
        st = {}

        def mm():
            st["mix"] = _dot(a_s[rows, :], w_o_ref[...]) + b_o_ref[...]

        def norm():
            o_ref[rows, :] = _layer_norm(ALPHA * x_ref[rows, :] + st.pop("mix"), g_ref[...], b_ref[...])

        return mm, norm

    steps = [(n, j) for n in range(tm // blk) for j in range(ATTN_HEADS // 2)]
    pairs = ATTN_HEADS // 2
    extra = {}
    for n in range(tm // blk - 1):
        mm, norm = out_block(n)
        extra[pairs * (n + 1) + 3] = mm
        extra[pairs * (n + 1) + 6] = norm
    s_val = {}
    p_val = {}
    for i in range(len(steps) + 2):
        if i < len(steps):
            s_val[i] = scores(*steps[i])
        if i in extra:
            extra[i]()
        if 0 <= i - 1 < len(steps):
            p_val[i - 1] = probs(*steps[i - 1], s_val.pop(i - 1))
        if 0 <= i - 2 < len(steps):
            output(*steps[i - 2], *p_val.pop(i - 2))
    for task in out_block(tm // blk - 1):
        task()


def _ffn_kernel(x_ref, p_ref, wg_ref, wu_ref, wd_ref, g_ref, b_ref, wpg_ref, bpg_ref, wple_ref, o_ref):
    tm = x_ref.shape[0]
    n_blk = tm // FFN_ROWS

    def blk(i):
        rows = slice(i * FFN_ROWS, (i + 1) * FFN_ROWS)
        st = {}

        def gate_up():
            xb = x_ref[rows, :].astype(BF16)
            st["g"] = _dot(xb, wg_ref[...])
            st["u"] = _dot(xb, wu_ref[...])

        def act():
            st["h"] = (_silu(st.pop("g")) * st.pop("u")).astype(BF16)

        def down():
            st["y"] = _dot(st.pop("h"), wd_ref[...])

        def norm():
            st["x2"] = _layer_norm(ALPHA * x_ref[rows, :] + st.pop("y"), g_ref[...], b_ref[...])

        def ple():
            st["gate"] = _dot(st["x2"].astype(BF16), wpg_ref[...])
            st["pe"] = _dot(p_ref[rows, :].astype(BF16), wple_ref[...])

        def out():
            o_ref[rows, :] = st.pop("x2") + st.pop("pe") * _sigmoid(st.pop("gate") + bpg_ref[...])

        return [gate_up, act, down, norm, ple, out]

    stages = [blk(i) for i in range(n_blk)]
    n_st = len(stages[0])
    for step in range(0, n_st + 2 * (n_blk - 1), 2):
        for i in range(n_blk):
            k = step - 2 * i
            if 0 <= k < n_st:
                stages[i][k]()
        for i in range(n_blk):
            k = step - 2 * i + 1
            if 0 <= k < n_st:
                stages[i][k]()


def _with_casts(body, n_in, n_cast):
    def kernel_fn(*refs):
        ins, casts_in = refs[:n_in], refs[n_in:n_in + n_cast]
        out, casts_out = refs[n_in + n_cast], refs[n_in + n_cast + 1:n_in + 2 * n_cast + 1]
        scratch = refs[n_in + 2 * n_cast + 1:]
        body(*ins, out, *scratch)
        for src, dst in zip(casts_in, casts_out):
            dst[...] = src[...].astype(BF16)

    return kernel_fn


def _cast_plan(w, layer, steps, flat_step):
    _, rows, cols = w.shape
    per, reuse = rows // steps, 1
    while per % BF16_SUBLANES:
        per, reuse = per * 2, reuse * 2
    assert rows % per == 0
    in_spec = pl.BlockSpec((None, per, cols), lambda *g: (layer, flat_step(*g) // reuse, 0))
    out_spec = pl.BlockSpec((per, cols), lambda *g: (flat_step(*g) // reuse, 0))
    return in_spec, out_spec, jax.ShapeDtypeStruct((rows, cols), BF16)


def _resident(shape):
    return pl.BlockSpec(shape, lambda *_: (0,) * len(shape), pipeline_mode=pl.Buffered(1))


def _row(v):
    return v.reshape(1, -1).astype(F32)


def _pad_lanes(v, width):
    return jnp.pad(v, ((0, 0), (0, width - v.shape[1])))


def _mixer0(x2d, batch, w_in, conv_w, conv_b, dt_bias, a_log, d_skip, norm_w, sc_conv_w, w_out, ln_g, ln_b, next_w):
    t = x2d.shape[0]
    n_l = t // batch // TM_MIX0
    o1 = SSD_INNER
    o2 = o1 + SSD_XBC
    o3 = o2 + SSD_HEADS
    w_zx = w_in[:, :o2].astype(BF16)
    w_sc = w_in[:, o3:].astype(BF16)
    w_dt = _pad_lanes(jnp.tile(w_in[:, o2:o3], (1, DT_REP)), LANES).astype(BF16)
    dtb = _pad_lanes(jnp.tile(_row(dt_bias), (1, DT_REP)), LANES)
    alog = _pad_lanes(jnp.tile(_row(a_log), (1, DT_REP)), LANES)
    dexp = jnp.repeat(_row(d_skip), SSD_HEAD_DIM, axis=1)
    r = jnp.arange(LANES)[:, None]
    col = jnp.arange(DT_REP * SSD_INNER)[None, :]
    e1 = ((r < DT_REP * SSD_HEADS) & (col // SSD_HEAD_DIM == r)).astype(BF16)
    e2 = jnp.concatenate([e1, e1], axis=0)
    tri = (jnp.arange(SSD_CHUNK)[:, None] >= jnp.arange(SSD_CHUNK)[None, :]).astype(BF16)
    tril3 = jnp.concatenate([tri, tri, tri], axis=1)

    tok = lambda width: pl.BlockSpec((TM_MIX0, width), lambda b, l: (b * n_l + l, 0))
    plans = [_cast_plan(w, layer, batch * n_l, lambda b, l: b * n_l + l) for w, layer in next_w]
    in_specs = [tok(D_MODEL), _resident(w_zx.shape), _resident(w_sc.shape), _resident(w_dt.shape),
                _resident((SSD_CONV, SSD_XBC)),
                _resident((1, SSD_XBC)), _resident((1, LANES)), _resident((1, LANES)),
                _resident((1, SSD_INNER)), _resident((1, SSD_INNER)), _resident((SC_CONV, SC_DIM)),
                _resident((MIX_OUT, D_MODEL)), _resident((1, D_MODEL)), _resident((1, D_MODEL)),
                _resident(e2.shape), _resident(tril3.shape)]
    outs = pl.pallas_call(
        _with_casts(_mixer0_kernel, len(in_specs), len(plans)),
        grid=(batch, n_l),
        in_specs=in_specs + [p[0] for p in plans],
        out_specs=[tok(D_MODEL)] + [p[1] for p in plans],
        out_shape=[jax.ShapeDtypeStruct((t, D_MODEL), F32)] + [p[2] for p in plans],
        scratch_shapes=[pltpu.VMEM((TM_MIX0, D_MODEL), BF16),
                        pltpu.VMEM((TM_MIX0, SSD_INNER), F32), pltpu.VMEM((TM_MIX0, SSD_XBC), F32),
                        pltpu.VMEM((TM_MIX0, LANES), F32), pltpu.VMEM((SSD_CHUNK, TM_MIX0), F32),
                        pltpu.VMEM((TM_MIX0, MIX_OUT), BF16),
                        pltpu.VMEM((SUBLANES, SSD_XBC), F32), pltpu.VMEM((SUBLANES, SC_DIM), F32),
                        pltpu.VMEM((SSD_GROUPS, SSD_STATE, SSD_HPG * SSD_HEAD_DIM), F32)],
        compiler_params=pltpu.CompilerParams(dimension_semantics=("arbitrary", "arbitrary"),
                                             vmem_limit_bytes=VMEM_LIMIT_BYTES),
        name="mixer0_ssd_shortconv",
    )(x2d, w_zx, w_sc, w_dt, conv_w.astype(F32), _row(conv_b), dtb, alog, dexp, _row(norm_w), sc_conv_w.astype(F32),
      w_out.astype(BF16), _row(ln_g), _row(ln_b), e2, tril3, *[w for w, _ in next_w])
    return outs[0], outs[1:]


def _attn(x2d, batch, w_qkv, b_qkv, sinks, w_o, b_o, ln_g, ln_b, next_w):
    t = x2d.shape[0]
    n_l = t // batch // TM_ATTN
    nq = ATTN_HEADS * ATTN_HEAD_DIM
    nkv = ATTN_KV_HEADS * ATTN_HEAD_DIM

    width = nq + 2 * nkv
    ones_blk = (jnp.arange(4 * ATTN_BLOCK)[:, None] // (2 * ATTN_BLOCK) == jnp.arange(LANES)[None, :] // ATTN_HEAD_DIM
                ).astype(BF16)
    tok = pl.BlockSpec((TM_ATTN, D_MODEL), lambda b, l: (b * n_l + l, 0))
    plans = [_cast_plan(w, layer, batch * n_l, lambda b, l: b * n_l + l) for w, layer in next_w]
    in_specs = [pl.BlockSpec(memory_space=pltpu.SMEM), tok, _resident((D_MODEL, width)), _resident((1, width)),
                _resident((nq, D_MODEL)), _resident((1, D_MODEL)), _resident((1, D_MODEL)),
                _resident((1, D_MODEL)), _resident(ones_blk.shape)]
    outs = pl.pallas_call(
        _with_casts(_attn_kernel, len(in_specs), len(plans)),
        grid=(batch, n_l),
        in_specs=in_specs + [p[0] for p in plans],
        out_specs=[tok] + [p[1] for p in plans],
        out_shape=[jax.ShapeDtypeStruct((t, D_MODEL), F32)] + [p[2] for p in plans],
        scratch_shapes=[pltpu.VMEM((TM_ATTN, D_MODEL), BF16),
                        pltpu.VMEM((TM_ATTN, nq), BF16), pltpu.VMEM((4 * nkv, TM_ATTN + ATTN_BLOCK), BF16),
                        pltpu.VMEM((TM_ATTN + ATTN_BLOCK, 4 * nkv), BF16), pltpu.VMEM((TM_ATTN, nq), BF16),
                        pltpu.VMEM((4 * nkv, ATTN_BLOCK), BF16), pltpu.VMEM((ATTN_BLOCK, 4 * nkv), BF16)],
        compiler_params=pltpu.CompilerParams(dimension_semantics=("arbitrary", "arbitrary"),
                                             vmem_limit_bytes=VMEM_LIMIT_BYTES),
        name="mixer1_swa",
    )(sinks.astype(F32), x2d, w_qkv, _row(b_qkv), w_o, _row(b_o), _row(ln_g), _row(ln_b), ones_blk,
      *[w for w, _ in next_w])
    return outs[0], outs[1:]


def _ffn(x2d, layer, p3d, w_gate, w_up, w_down, ln_g, ln_b, w_ple_gate, b_ple_gate, w_ple, next_w):
    t = x2d.shape[0]
    steps = t // TM_FFN
    tok = lambda width: pl.BlockSpec((TM_FFN, width), lambda i: (i, 0))
    plans = [_cast_plan(w, lyr, steps, lambda i: i) for w, lyr in next_w]
    in_specs = [tok(D_MODEL), pl.BlockSpec((None, TM_FFN, PLE_DIM), lambda i: (layer, i, 0)),
                _resident((D_MODEL, D_FF)), _resident((D_MODEL, D_FF)), _resident((D_FF, D_MODEL)),
                _resident((1, D_MODEL)), _resident((1, D_MODEL)),
                _resident((D_MODEL, D_MODEL)), _resident((1, D_MODEL)), _resident((PLE_DIM, D_MODEL))]
    outs = pl.pallas_call(
        _with_casts(_ffn_kernel, len(in_specs), len(plans)),
        grid=(steps,),
        in_specs=in_specs + [p[0] for p in plans],
        out_specs=[tok(D_MODEL)] + [p[1] for p in plans],
        out_shape=[jax.ShapeDtypeStruct((t, D_MODEL), F32)] + [p[2] for p in plans],
        compiler_params=pltpu.CompilerParams(dimension_semantics=("arbitrary",),
                                             vmem_limit_bytes=VMEM_LIMIT_BYTES),
        name="ffn_ple",
    )(x2d, p3d, w_gate, w_up, w_down, _row(ln_g[layer]), _row(ln_b[layer]),
      w_ple_gate, _row(b_ple_gate[layer]), w_ple, *[w for w, _ in next_w])
    return outs[0], outs[1:]


def kernel(x, p, w_in_mix, ssd_conv_w, ssd_conv_b, ssd_dt_bias, ssd_a_log, ssd_d, ssd_norm_w, sc_conv_w, w_out_mix, w_qkv, b_qkv, attn_sinks, w_o, b_o, ln_mix_g, ln_mix_b, w_ffn_gate, w_ffn_up, w_ffn_down, ln_ffn_g, ln_ffn_b, w_ple, w_ple_gate, b_ple_gate):
    batch, seq, d = x.shape
    assert d == D_MODEL and seq % TM_MIX0 == 0 and seq % TM_ATTN == 0 and (batch * seq) % TM_FFN == 0
    assert p.shape == (DEPTH, batch, seq, PLE_DIM)
    h = x.reshape(batch * seq, d)
    p2d = p.reshape(DEPTH, batch * seq, PLE_DIM)

    def ffn_weights(i):
        return [(w, i) for w in (w_ffn_gate, w_ffn_up, w_ffn_down, w_ple_gate, w_ple)]

    h, ffn0_w = _mixer0(h, batch, w_in_mix[0], ssd_conv_w[0], ssd_conv_b[0], ssd_dt_bias[0], ssd_a_log[0], ssd_d[0],
                        ssd_norm_w[0], sc_conv_w[0], w_out_mix[0], ln_mix_g[0], ln_mix_b[0], ffn_weights(0))
    h, attn_w = _ffn(h, 0, p2d, *ffn0_w[:3], ln_ffn_g, ln_ffn_b, ffn0_w[3], b_ple_gate, ffn0_w[4],
                     [(w_qkv, 0), (w_o, 0)])
    h, ffn1_w = _attn(h, batch, attn_w[0], b_qkv[0], attn_sinks[0], attn_w[1], b_o[0], ln_mix_g[1], ln_mix_b[1],
                      ffn_weights(1))
    h, _ = _ffn(h, 1, p2d, *ffn1_w[:3], ln_ffn_g, ln_ffn_b, ffn1_w[3], b_ple_gate, ffn1_w[4], [])
    return h.reshape(batch, seq, d)
```

```python
import functools

import jax
import jax.numpy as jnp
from jax import lax
from jax.experimental import pallas as pl
from jax.experimental.pallas import tpu as pltpu

F32 = jnp.float32
BF16 = jnp.bfloat16

D_MODEL = 1024
DEPTH = 2
PLE_DIM = 256
SSD_HEADS = 16
SSD_HEAD_DIM = 64
SSD_INNER = SSD_HEADS * SSD_HEAD_DIM
SSD_GROUPS = 2
SSD_HPG = SSD_HEADS // SSD_GROUPS
SSD_STATE = 128
SSD_CONV = 4
SSD_CHUNK = 128
SSD_XBC = SSD_INNER + 2 * SSD_GROUPS * SSD_STATE
SC_DIM = 1024
SC_CONV = 3
MIX_OUT = SSD_INNER + SC_DIM
ATTN_HEADS = 16
ATTN_KV_HEADS = 4
ATTN_HEAD_DIM = 64
ATTN_BLOCK = 128
D_FF = ((8 * D_MODEL + 3 * 256 - 1) // (3 * 256)) * 256
ALPHA = (2 * DEPTH) ** 0.25
LN_EPS = 1e-5
LOG2E = 1.4426950408889634
RMS_EPS = 1e-5

LANES = 128
SUBLANES = 8
BF16_SUBLANES = 16
V7X_VMEM_BYTES = 64 * 1024 * 1024
VMEM_LIMIT_BYTES = V7X_VMEM_BYTES - 8 * 1024 * 1024

TM_MIX0 = 512
TM_ATTN = 1024
TM_FFN = 1024
FFN_ROWS = 256

_C_Z = 0
_C_XBC = _C_Z + SSD_INNER
_C_SCB = 0
_C_SCC = _C_SCB + SC_DIM
_C_SCH = _C_SCC + SC_DIM
_W_DT = SSD_INNER + SSD_XBC
_W_SC = _W_DT + SSD_HEADS
DT_REP = 2
COL_CHUNK = 256


def _dot(a, b):
    return jnp.dot(a, b, preferred_element_type=F32)


def _dot_nt(a, b):
    return lax.dot_general(a, b, (((1,), (1,)), ((), ())), preferred_element_type=F32)


def _dot_tn(a, b):
    return lax.dot_general(a, b, (((0,), (0,)), ((), ())), preferred_element_type=F32)


def _sigmoid(x):
    return 1.0 / (1.0 + jnp.exp(-x))


def _silu(x):
    return x * _sigmoid(x)


def _layer_norm(r, g, b):
    mu = jnp.mean(r, axis=-1, keepdims=True)
    xc = r - mu
    var = jnp.mean(xc * xc, axis=-1, keepdims=True)
    return xc * lax.rsqrt(var + LN_EPS) * g + b


def _split_bf16(v, parts):
    out = []
    rem = v
    for i in range(parts):
        p = rem.astype(BF16)
        out.append(p)
        if i + 1 < parts:
            rem = rem - p.astype(F32)
    return out


def _causal_conv(cur, prev_tail, w):
    width, cols = w.shape
    rows = lax.broadcasted_iota(jnp.int32, (SUBLANES, cols), 0)
    acc = cur * w[width - 1:width, :]
    for s in range(1, width):
        rolled = pltpu.roll(cur, s, 0)
        head = jnp.where(rows < s, pltpu.roll(prev_tail, s, 0), rolled[0:SUBLANES])
        shifted = jnp.concatenate([head, rolled[SUBLANES:]], axis=0)
        acc = acc + shifted * w[width - 1 - s:width - s, :]
    return acc


def _mixer0_kernel(x_ref, w_in_ref, cw_ref, cb_ref, dtb_ref, alog_ref, dexp_ref, nw_ref, scw_ref, w_out_ref,
                   g_ref, b_ref, e2_ref, tril_ref, o_ref,
                   xb_s, z_s, xbc_s, dt_s, acs_s, ycat_s, prev_xbc_s, prev_sc_s, h_s, w_sc_s, w_dt_s):
    tm = x_ref.shape[0]
    c = SSD_CHUNK
    gw = SSD_HPG * SSD_HEAD_DIM

    @pl.when(pl.program_id(1) == 0)
    def _():
        prev_xbc_s[...] = jnp.zeros_like(prev_xbc_s)
        prev_sc_s[...] = jnp.zeros_like(prev_sc_s)
        h_s[...] = jnp.zeros_like(h_s)

    @pl.when(jnp.logical_and(pl.program_id(0) == 0, pl.program_id(1) == 0))
    def _():
        for k in range(3 * SC_DIM // COL_CHUNK):
            cols = slice(k * COL_CHUNK, (k + 1) * COL_CHUNK)
            w_sc_s[:, cols] = w_in_ref[:, _W_SC + k * COL_CHUNK:_W_SC + (k + 1) * COL_CHUNK]
        blk = w_in_ref[:, _W_DT:_W_DT + LANES].astype(F32)
        lane = lax.broadcasted_iota(jnp.int32, blk.shape, 1)
        rep = jnp.where(lane < DT_REP * SSD_HEADS, pltpu.roll(blk, SSD_HEADS, 1), 0.0)
        w_dt_s[...] = jnp.where(lane < SSD_HEADS, blk, rep).astype(BF16)

    xb_s[...] = x_ref[...].astype(BF16)

    def proj(w_ref, lo, width):
        return _dot(xb_s[...], w_ref[:, lo:lo + width])

    def xbc_piece(k):
        cols = slice(k * COL_CHUNK, (k + 1) * COL_CHUNK)
        st = {}

        def mm():
            st["raw"] = proj(w_in_ref, _C_XBC + k * COL_CHUNK, COL_CHUNK)

        def vpu():
            raw = st.pop("raw")
            xbc_s[:, cols] = _silu(_causal_conv(raw, prev_xbc_s[:, cols], cw_ref[:, cols]) + cb_ref[:, cols])
            prev_xbc_s[:, cols] = raw[tm - SUBLANES:tm]

        return mm, vpu

    def dt_piece():
        st = {}

        def mm():
            st["raw"] = proj(w_dt_s, 0, LANES)

        def vpu():
            pre = st.pop("raw") + dtb_ref[...]
            dt = jnp.maximum(pre, 0.0) + jnp.log1p(jnp.exp(-jnp.abs(pre)))
            dt_s[...] = dt
            lane_row = lax.broadcasted_iota(jnp.int32, (1, LANES), 1)
            da = dt * jnp.where(lane_row < DT_REP * SSD_HEADS, -jnp.exp(alog_ref[...]), 0.0)
            da = jnp.concatenate([da[i * c:(i + 1) * c] for i in range(tm // c)], axis=1)
            acs_s[...] = _dot(tril_ref[...], jnp.concatenate(_split_bf16(da, 3), axis=0))

        return mm, vpu

    def z_piece(k):
        def mm():
            z_s[:, k * COL_CHUNK:(k + 1) * COL_CHUNK] = proj(w_in_ref, _C_Z + k * COL_CHUNK, COL_CHUNK)

        return mm

    def sc_piece(k):
        cols = slice(k * COL_CHUNK, (k + 1) * COL_CHUNK)
        st = {}

        def mm_b():
            st["b"] = proj(w_sc_s, _C_SCB + k * COL_CHUNK, COL_CHUNK)

        def mm_c():
            st["c"] = proj(w_sc_s, _C_SCC + k * COL_CHUNK, COL_CHUNK)

        def mm_h():
            st["u"] = st.pop("c") * proj(w_sc_s, _C_SCH + k * COL_CHUNK, COL_CHUNK)

        def vpu():
            u = st.pop("u")
            y_sc = st.pop("b") * _causal_conv(u, prev_sc_s[:, cols], scw_ref[:, cols])
            ycat_s[:, SSD_INNER + k * COL_CHUNK:SSD_INNER + (k + 1) * COL_CHUNK] = y_sc.astype(BF16)
            prev_sc_s[:, cols] = u[tm - SUBLANES:tm]

        return mm_b, mm_c, mm_h, vpu

    def out_piece(i):
        rows = slice(i * c, (i + 1) * c)
        st = {}

        def sc_half():
            st["mix"] = _dot(ycat_s[rows, SSD_INNER:], w_out_ref[SSD_INNER:, :])

        def ssd_half():
            mix = st.pop("mix") + _dot(ycat_s[rows, :SSD_INNER], w_out_ref[:SSD_INNER, :])
            o_ref[rows, :] = _layer_norm(ALPHA * x_ref[rows, :] + mix, g_ref[...], b_ref[...])

        return sc_half, ssd_half

    def ssd_chunk(ci):
        rows = slice(ci * c, (ci + 1) * c)
        st = {}

        def s1():
            lane_sq = lax.broadcasted_iota(jnp.int32, (c, LANES), 1)
            dt = dt_s[rows, :]
            a_cs = acs_s[:, ci * LANES:(ci + 1) * LANES]
            ecs = jnp.exp(a_cs)
            dtdte = dt * jnp.exp(a_cs[c - 1:c, :] - a_cs)
            q = jnp.where(lane_sq < SSD_HEADS, dtdte, ecs)
            ex = _dot(jnp.concatenate(_split_bf16(q, 2), axis=1), e2_ref[...])
            st["dtdte_e"] = ex[:, :SSD_INNER]
            st["ecs_e"] = ex[:, SSD_INNER:]
            st["a_cs"] = a_cs
            st["t"] = jnp.where(lane_sq < SSD_HEADS, a_cs, dt).T

        def s2():
            group_lane = lax.broadcasted_iota(jnp.int32, (c, SSD_GROUPS * SSD_STATE), 1) // SSD_STATE
            bmf = xbc_s[rows, SSD_INNER:SSD_INNER + SSD_GROUPS * SSD_STATE]
            bm = bmf.astype(BF16)
            cm = xbc_s[rows, SSD_INNER + SSD_GROUPS * SSD_STATE:].astype(BF16)
            st["bm"] = bm
            st["cm"] = cm
            b_diag = jnp.concatenate([jnp.where(group_lane == g, bmf, 0.0) for g in range(SSD_GROUPS)], axis=0)
            st["cb"] = _dot_nt(cm, b_diag.astype(BF16))
            st["y"] = []

        def s2_quad(qi):
            lane_sq = lax.broadcasted_iota(jnp.int32, (c, LANES), 1)
            causal = lax.broadcasted_iota(jnp.int32, (c, LANES), 0) >= lane_sq
            quad_lane = lax.broadcasted_iota(jnp.int32, (c, 4 * SSD_HEAD_DIM), 1) // SSD_HEAD_DIM
            g = (4 * qi) // SSD_HPG
            h0 = 4 * qi
            a_cs = st["a_cs"]
            t = st["t"]
            cb = st["cb"][:, g * c:(g + 1) * c]
            ms = []
            for h in range(h0, h0 + 4):
                seg = jnp.broadcast_to(a_cs[:, h:h + 1], (c, LANES)) - t[h:h + 1, :]
                dec = jnp.exp(jnp.where(causal, seg, -jnp.inf))
                ms.append((cb * dec * t[SSD_HEADS + h:SSD_HEADS + h + 1, :]).astype(BF16))
            xq = xbc_s[rows, h0 * SSD_HEAD_DIM:(h0 + 4) * SSD_HEAD_DIM]
            rhs = jnp.concatenate([jnp.where(quad_lane == k, xq, 0.0) for k in range(4)], axis=0)
            st["y"].append(_dot(jnp.concatenate(ms, axis=1), rhs.astype(BF16)))

        def s3():
            for key in ("a_cs", "t", "cb"):
                st.pop(key)
            bm = st.pop("bm")
            cm = st.pop("cm")
            ecs_e = st["ecs_e"]
            dtdte_e = st.pop("dtdte_e")
            y_off = []
            for g in range(SSD_GROUPS):
                hg = h_s[g]
                y_off.append(_dot(cm[:, g * SSD_STATE:(g + 1) * SSD_STATE], hg.astype(BF16)))
                xd = (xbc_s[rows, g * gw:(g + 1) * gw] * dtdte_e[:, g * gw:(g + 1) * gw]).astype(BF16)
                new = _dot_tn(bm[:, g * SSD_STATE:(g + 1) * SSD_STATE], xd)
                h_s[g] = hg * ecs_e[c - 1:c, g * gw:(g + 1) * gw] + new
            st["y_off"] = jnp.concatenate(y_off, axis=1)

        def s4():
            y = jnp.concatenate(st.pop("y"), axis=1) + st.pop("y_off") * st.pop("ecs_e")
            y = y + xbc_s[rows, :SSD_INNER] * dexp_ref[...]
            y = y * _silu(z_s[rows, :])
            yn = []
            for g in range(SSD_GROUPS):
                yg = y[:, g * gw:(g + 1) * gw]
                yn.append(yg * lax.rsqrt(jnp.mean(yg * yg, axis=-1, keepdims=True) + RMS_EPS))
            ycat_s[rows, 0:SSD_INNER] = (jnp.concatenate(yn, axis=1) * nw_ref[...]).astype(BF16)

        quads = [functools.partial(s2_quad, qi) for qi in range(SSD_HEADS // 4)]
        return [s1, s2] + quads + [s3, s4]

    head = [xbc_piece(k) for k in range(SSD_XBC // COL_CHUNK)] + [dt_piece()]
    z_tasks = [z_piece(k) for k in range(SSD_INNER // COL_CHUNK)]
    pending = None
    for mm, vpu in head:
        mm()
        if pending is not None:
            if z_tasks:
                z_tasks.pop(0)()
            pending()
        pending = vpu
    pending()

    n_chunks = tm // c
    fillers = [(0, t) for t in z_tasks]
    for k in range(SC_DIM // COL_CHUNK):
        fillers.extend((0, t) for t in sc_piece(k))
    outs = [out_piece(i) for i in range(n_chunks)]
    fillers.extend((0, sc_half) for sc_half, _ in outs)
    per_chunk = len(ssd_chunk(0))
    fillers.extend((per_chunk * (i + 1), ssd_half) for i, (_, ssd_half) in enumerate(outs))
    stages = [s for ci in range(n_chunks) for s in ssd_chunk(ci)]
    n_fill = len(fillers)
    for i, stage in enumerate(stages):
        want = ((i + 1) * n_fill) // len(stages) - (n_fill - len(fillers))
        while want > 0 and fillers and fillers[0][0] <= i:
            fillers.pop(0)[1]()
            want -= 1
        stage()
    for _, task in fillers:
        task()


def _attn_kernel(sink_ref, x_ref, w_qkv_ref, b_qkv_ref, w_o_ref, b_o_ref, g_ref, b_ref, ones_ref, o_ref,
                 xb_s, q_s, kt_s, v_s, a_s, kprev_s, vprev_s):
    tm = x_ref.shape[0]
    blk = ATTN_BLOCK
    first_tile = pl.program_id(1) == 0
    nq = ATTN_HEADS * ATTN_HEAD_DIM
    nkv = ATTN_KV_HEADS * ATTN_HEAD_DIM
    nkv2 = ATTN_KV_HEADS * LANES
    half = ATTN_HEAD_DIM

    @pl.when(first_tile)
    def _():
        kprev_s[...] = jnp.zeros_like(kprev_s)
        vprev_s[...] = jnp.zeros_like(vprev_s)

    kt_s[:, :blk] = kprev_s[...]
    v_s[:blk, :] = vprev_s[...]
    xb_s[...] = x_ref[...].astype(BF16)

    def proj_piece(k):
        lo = k * COL_CHUNK
        st = {}

        def mm():
            st["v"] = _dot(xb_s[...], w_qkv_ref[:, lo:lo + COL_CHUNK]) + b_qkv_ref[:, lo:lo + COL_CHUNK]

        def q_vpu():
            q_s[:, lo:lo + COL_CHUNK] = (st.pop("v") * (ATTN_HEAD_DIM ** -0.5 * LOG2E)).astype(BF16)

        def k_vpu():
            kt = st.pop("v").T.astype(BF16)
            zero = jnp.zeros((half, tm), BF16)
            for h in range(ATTN_KV_HEADS):
                kh = kt[h * half:(h + 1) * half]
                for base, part in ((0, jnp.concatenate([kh, zero], axis=0)),
                                   (nkv2, jnp.concatenate([zero, kh], axis=0))):
                    kt_s[base + h * LANES:base + (h + 1) * LANES, blk:] = part
                    kprev_s[base + h * LANES:base + (h + 1) * LANES, :] = part[:, tm - blk:]

        def v_vpu():
            vd = st.pop("v")
            lo_c = lax.broadcasted_iota(jnp.int32, (tm, LANES), 1) < half
            for c2 in range(ATTN_KV_HEADS // 2):
                col = vd[:, c2 * LANES:(c2 + 1) * LANES]
                swapped = pltpu.roll(col, half, 1)
                planes = ((0, 2 * c2, jnp.where(lo_c, col, 0.0)), (nkv2, 2 * c2, jnp.where(lo_c, 0.0, swapped)),
                          (0, 2 * c2 + 1, jnp.where(lo_c, swapped, 0.0)), (nkv2, 2 * c2 + 1, jnp.where(lo_c, 0.0, col)))
                for base, h, part in planes:
                    part = part.astype(BF16)
                    v_s[blk:, base + h * LANES:base + (h + 1) * LANES] = part
                    vprev_s[:, base + h * LANES:base + (h + 1) * LANES] = part[tm - blk:]

        vpu = q_vpu if lo < nq else (k_vpu if lo < nq + nkv else v_vpu)
        return mm, vpu

    pending = None
    for k in range((nq + 2 * nkv) // COL_CHUNK):
        mm, vpu = proj_piece(k)
        mm()
        if pending is not None:
            pending()
        pending = vpu
    pending()

    qi = lax.broadcasted_iota(jnp.int32, (blk, 2 * blk), 0)
    kj = lax.broadcasted_iota(jnp.int32, (blk, 2 * blk), 1)
    band = (kj > qi) & (kj <= qi + blk)
    band0 = band & (kj >= jnp.where(first_tile, blk, 0))
    lane_o = lax.broadcasted_iota(jnp.int32, (blk, LANES), 1) < half

    def scores(n, j):
        c0 = (j // 2) * LANES
        keys = slice(n * blk, (n + 2) * blk)
        kk = jnp.concatenate([kt_s[c0:c0 + LANES, keys], kt_s[nkv2 + c0:nkv2 + c0 + LANES, keys]], axis=1)
        return _dot(q_s[n * blk:(n + 1) * blk, j * LANES:(j + 1) * LANES], kk)

    def probs(n, j, s):
        valid = band0 if n == 0 else band
        ps = []
        es = []
        for t in range(2):
            sink = sink_ref[2 * j + t] * LOG2E
            sh = jnp.where(valid, s[:, t * 2 * blk:(t + 1) * 2 * blk], -jnp.inf)
            m = jnp.maximum(jnp.max(sh, axis=-1, keepdims=True), sink)
            ps.append(jnp.exp2(sh - m).astype(BF16))
            es.append(jnp.exp2(sink - m))
        return jnp.concatenate(ps, axis=1), jnp.where(lane_o, es[0], es[1])

    def output(n, j, p, es):
        c0 = (j // 2) * LANES
        keys = slice(n * blk, (n + 2) * blk)
        vv = jnp.concatenate([v_s[keys, c0:c0 + LANES], v_s[keys, nkv2 + c0:nkv2 + c0 + LANES]], axis=0)
        r = _dot(p, jnp.concatenate([vv, ones_ref[...]], axis=1))
        a_s[n * blk:(n + 1) * blk, j * LANES:(j + 1) * LANES] = (r[:, :LANES] / (r[:, LANES:] + es)).astype(BF16)

    def out_block(n):
        rows = slice(n * blk, (n + 1) * blk)
        st = {}

        def mm():
            st["mix"] = _dot(a_s[rows, :], w_o_ref[...]) + b_o_ref[...]

        def norm():
            o_ref[rows, :] = _layer_norm(ALPHA * x_ref[rows, :] + st.pop("mix"), g_ref[...], b_ref[...])

        return mm, norm

    steps = [(n, j) for n in range(tm // blk) for j in range(ATTN_HEADS // 2)]
    pairs = ATTN_HEADS // 2
    extra = {}
    for n in range(tm // blk - 1):
        mm, norm = out_block(n)
        extra[pairs * (n + 1) + 3] = mm
        extra[pairs * (n + 1) + 6] = norm
    s_val = {}
    p_val = {}
    for i in range(len(steps) + 2):
        if i < len(steps):
            s_val[i] = scores(*steps[i])
        if i in extra:
            extra[i]()
        if 0 <= i - 1 < len(steps):
            p_val[i - 1] = probs(*steps[i - 1], s_val.pop(i - 1))
        if 0 <= i - 2 < len(steps):
            output(*steps[i - 2], *p_val.pop(i - 2))
    for task in out_block(tm // blk - 1):
        task()


def _ffn_kernel(x_ref, p_ref, wg_ref, wu_ref, wd_ref, g_ref, b_ref, wpg_ref, bpg_ref, wple_ref, o_ref):
    tm = x_ref.shape[0]
    n_blk = tm // FFN_ROWS

    def blk(i):
        rows = slice(i * FFN_ROWS, (i + 1) * FFN_ROWS)
        st = {}

        def gate_up():
            xb = x_ref[rows, :].astype(BF16)
            st["g"] = _dot(xb, wg_ref[...])
            st["u"] = _dot(xb, wu_ref[...])

        def act():
            st["h"] = (_silu(st.pop("g")) * st.pop("u")).astype(BF16)

        def down():
            st["y"] = _dot(st.pop("h"), wd_ref[...])

        def norm():
            st["x2"] = _layer_norm(ALPHA * x_ref[rows, :] + st.pop("y"), g_ref[...], b_ref[...])

        def ple():
            st["gate"] = _dot(st["x2"].astype(BF16), wpg_ref[...])
            st["pe"] = _dot(p_ref[rows, :].astype(BF16), wple_ref[...])

        def out():
            o_ref[rows, :] = st.pop("x2") + st.pop("pe") * _sigmoid(st.pop("gate") + bpg_ref[...])

        return [gate_up, act, down, norm, ple, out]

    stages = [blk(i) for i in range(n_blk)]
    n_st = len(stages[0])
    for step in range(0, n_st + 2 * (n_blk - 1), 2):
        for i in range(n_blk):
            k = step - 2 * i
            if 0 <= k < n_st:
                stages[i][k]()
        for i in range(n_blk):
            k = step - 2 * i + 1
            if 0 <= k < n_st:
                stages[i][k]()


def _with_casts(body, n_in, n_cast):
    def kernel_fn(*refs):
        ins, casts_in = refs[:n_in], refs[n_in:n_in + n_cast]
        out, casts_out = refs[n_in + n_cast], refs[n_in + n_cast + 1:n_in + 2 * n_cast + 1]
        scratch = refs[n_in + 2 * n_cast + 1:]
        body(*ins, out, *scratch)
        for src, dst in zip(casts_in, casts_out):
            dst[...] = src[...].astype(BF16)

    return kernel_fn


def _cast_plan(w, layer, steps, flat_step):
    _, rows, cols = w.shape
    per, reuse = rows // steps, 1
    while per % BF16_SUBLANES:
        per, reuse = per * 2, reuse * 2
    assert rows % per == 0
    in_spec = pl.BlockSpec((None, per, cols), lambda *g: (layer, flat_step(*g) // reuse, 0))
    out_spec = pl.BlockSpec((per, cols), lambda *g: (flat_step(*g) // reuse, 0))
    return in_spec, out_spec, jax.ShapeDtypeStruct((rows, cols), BF16)


def _resident(shape):
    return pl.BlockSpec(shape, lambda *_: (0,) * len(shape), pipeline_mode=pl.Buffered(1))


def _row(v):
    return v.reshape(1, -1).astype(F32)


def _pad_lanes(v, width):
    return jnp.pad(v, ((0, 0), (0, width - v.shape[1])))


def _mixer0(x2d, batch, w_in, conv_w, conv_b, dt_bias, a_log, d_skip, norm_w, sc_conv_w, w_out, ln_g, ln_b, next_w):
    t = x2d.shape[0]
    n_l = t // batch // TM_MIX0
    w_in = w_in.astype(BF16)
    dtb = _pad_lanes(jnp.tile(_row(dt_bias), (1, DT_REP)), LANES)
    alog = _pad_lanes(jnp.tile(_row(a_log), (1, DT_REP)), LANES)
    dexp = jnp.repeat(_row(d_skip), SSD_HEAD_DIM, axis=1)
    r = jnp.arange(LANES)[:, None]
    col = jnp.arange(DT_REP * SSD_INNER)[None, :]
    e1 = ((r < DT_REP * SSD_HEADS) & (col // SSD_HEAD_DIM == r)).astype(BF16)
    e2 = jnp.concatenate([e1, e1], axis=0)
    tri = (jnp.arange(SSD_CHUNK)[:, None] >= jnp.arange(SSD_CHUNK)[None, :]).astype(BF16)
    tril3 = jnp.concatenate([tri, tri, tri], axis=1)

    tok = lambda width: pl.BlockSpec((TM_MIX0, width), lambda b, l: (b * n_l + l, 0))
    plans = [_cast_plan(w, layer, batch * n_l, lambda b, l: b * n_l + l) for w, layer in next_w]
    in_specs = [tok(D_MODEL), _resident(w_in.shape), _resident((SSD_CONV, SSD_XBC)),
                _resident((1, SSD_XBC)), _resident((1, LANES)), _resident((1, LANES)),
                _resident((1, SSD_INNER)), _resident((1, SSD_INNER)), _resident((SC_CONV, SC_DIM)),
                _resident((MIX_OUT, D_MODEL)), _resident((1, D_MODEL)), _resident((1, D_MODEL)),
                _resident(e2.shape), _resident(tril3.shape)]
    outs = pl.pallas_call(
        _with_casts(_mixer0_kernel, len(in_specs), len(plans)),
        grid=(batch, n_l),
        in_specs=in_specs + [p[0] for p in plans],
        out_specs=[tok(D_MODEL)] + [p[1] for p in plans],
        out_shape=[jax.ShapeDtypeStruct((t, D_MODEL), F32)] + [p[2] for p in plans],
        scratch_shapes=[pltpu.VMEM((TM_MIX0, D_MODEL), BF16),
                        pltpu.VMEM((TM_MIX0, SSD_INNER), F32), pltpu.VMEM((TM_MIX0, SSD_XBC), F32),
                        pltpu.VMEM((TM_MIX0, LANES), F32), pltpu.VMEM((SSD_CHUNK, TM_MIX0), F32),
                        pltpu.VMEM((TM_MIX0, MIX_OUT), BF16),
                        pltpu.VMEM((SUBLANES, SSD_XBC), F32), pltpu.VMEM((SUBLANES, SC_DIM), F32),
                        pltpu.VMEM((SSD_GROUPS, SSD_STATE, SSD_HPG * SSD_HEAD_DIM), F32),
                        pltpu.VMEM((D_MODEL, 3 * SC_DIM), BF16), pltpu.VMEM((D_MODEL, LANES), BF16)],
        compiler_params=pltpu.CompilerParams(dimension_semantics=("arbitrary", "arbitrary"),
                                             vmem_limit_bytes=VMEM_LIMIT_BYTES),
        name="mixer0_ssd_shortconv",
    )(x2d, w_in, conv_w.astype(F32), _row(conv_b), dtb, alog, dexp, _row(norm_w), sc_conv_w.astype(F32),
      w_out.astype(BF16), _row(ln_g), _row(ln_b), e2, tril3, *[w for w, _ in next_w])
    return outs[0], outs[1:]


def _attn(x2d, batch, w_qkv, b_qkv, sinks, w_o, b_o, ln_g, ln_b, next_w):
    t = x2d.shape[0]
    n_l = t // batch // TM_ATTN
    nq = ATTN_HEADS * ATTN_HEAD_DIM
    nkv = ATTN_KV_HEADS * ATTN_HEAD_DIM

    width = nq + 2 * nkv
    ones_blk = (jnp.arange(4 * ATTN_BLOCK)[:, None] // (2 * ATTN_BLOCK) == jnp.arange(LANES)[None, :] // ATTN_HEAD_DIM
                ).astype(BF16)
    tok = pl.BlockSpec((TM_ATTN, D_MODEL), lambda b, l: (b * n_l + l, 0))
    plans = [_cast_plan(w, layer, batch * n_l, lambda b, l: b * n_l + l) for w, layer in next_w]
    in_specs = [pl.BlockSpec(memory_space=pltpu.SMEM), tok, _resident((D_MODEL, width)), _resident((1, width)),
                _resident((nq, D_MODEL)), _resident((1, D_MODEL)), _resident((1, D_MODEL)),
                _resident((1, D_MODEL)), _resident(ones_blk.shape)]
    outs = pl.pallas_call(
        _with_casts(_attn_kernel, len(in_specs), len(plans)),
        grid=(batch, n_l),
        in_specs=in_specs + [p[0] for p in plans],
        out_specs=[tok] + [p[1] for p in plans],
        out_shape=[jax.ShapeDtypeStruct((t, D_MODEL), F32)] + [p[2] for p in plans],
        scratch_shapes=[pltpu.VMEM((TM_ATTN, D_MODEL), BF16),
                        pltpu.VMEM((TM_ATTN, nq), BF16), pltpu.VMEM((4 * nkv, TM_ATTN + ATTN_BLOCK), BF16),
                        pltpu.VMEM((TM_ATTN + ATTN_BLOCK, 4 * nkv), BF16), pltpu.VMEM((TM_ATTN, nq), BF16),
                        pltpu.VMEM((4 * nkv, ATTN_BLOCK), BF16), pltpu.VMEM((ATTN_BLOCK, 4 * nkv), BF16)],
        compiler_params=pltpu.CompilerParams(dimension_semantics=("arbitrary", "arbitrary"),
                                             vmem_limit_bytes=VMEM_LIMIT_BYTES),
        name="mixer1_swa",
    )(sinks.astype(F32), x2d, w_qkv, _row(b_qkv), w_o, _row(b_o), _row(ln_g), _row(ln_b), ones_blk,
      *[w for w, _ in next_w])
    return outs[0], outs[1:]


def _ffn(x2d, layer, p3d, w_gate, w_up, w_down, ln_g, ln_b, w_ple_gate, b_ple_gate, w_ple, next_w):
    t = x2d.shape[0]
    steps = t // TM_FFN
    tok = lambda width: pl.BlockSpec((TM_FFN, width), lambda i: (i, 0))
    plans = [_cast_plan(w, lyr, steps, lambda i: i) for w, lyr in next_w]
    in_specs = [tok(D_MODEL), pl.BlockSpec((None, TM_FFN, PLE_DIM), lambda i: (layer, i, 0)),
                _resident((D_MODEL, D_FF)), _resident((D_MODEL, D_FF)), _resident((D_FF, D_MODEL)),
                _resident((1, D_MODEL)), _resident((1, D_MODEL)),
                _resident((D_MODEL, D_MODEL)), _resident((1, D_MODEL)), _resident((PLE_DIM, D_MODEL))]
    outs = pl.pallas_call(
        _with_casts(_ffn_kernel, len(in_specs), len(plans)),
        grid=(steps,),
        in_specs=in_specs + [p[0] for p in plans],
        out_specs=[tok(D_MODEL)] + [p[1] for p in plans],
        out_shape=[jax.ShapeDtypeStruct((t, D_MODEL), F32)] + [p[2] for p in plans],
        compiler_params=pltpu.CompilerParams(dimension_semantics=("arbitrary",),
                                             vmem_limit_bytes=VMEM_LIMIT_BYTES),
        name="ffn_ple",
    )(x2d, p3d, w_gate, w_up, w_down, _row(ln_g[layer]), _row(ln_b[layer]),
      w_ple_gate, _row(b_ple_gate[layer]), w_ple, *[w for w, _ in next_w])
    return outs[0], outs[1:]


def kernel(x, p, w_in_mix, ssd_conv_w, ssd_conv_b, ssd_dt_bias, ssd_a_log, ssd_d, ssd_norm_w, sc_conv_w, w_out_mix, w_qkv, b_qkv, attn_sinks, w_o, b_o, ln_mix_g, ln_mix_b, w_ffn_gate, w_ffn_up, w_ffn_down, ln_ffn_g, ln_ffn_b, w_ple, w_ple_gate, b_ple_gate):
    batch, seq, d = x.shape
    assert d == D_MODEL and seq % TM_MIX0 == 0 and seq % TM_ATTN == 0 and (batch * seq) % TM_FFN == 0
    assert p.shape == (DEPTH, batch, seq, PLE_DIM)
    h = x.reshape(batch * seq, d)
    p2d = p.reshape(DEPTH, batch * seq, PLE_DIM)

    def ffn_weights(i):
        return [(w, i) for w in (w_ffn_gate, w_ffn_up, w_ffn_down, w_ple_gate, w_ple)]

    h, ffn0_w = _mixer0(h, batch, w_in_mix[0], ssd_conv_w[0], ssd_conv_b[0], ssd_dt_bias[0], ssd_a_log[0], ssd_d[0],
                        ssd_norm_w[0], sc_conv_w[0], w_out_mix[0], ln_mix_g[0], ln_mix_b[0], ffn_weights(0))
    h, attn_w = _ffn(h, 0, p2d, *ffn0_w[:3], ln_ffn_g, ln_ffn_b, ffn0_w[3], b_ple_gate, ffn0_w[4],
                     [(w_qkv, 0), (w_o, 0)])
    h, ffn1_w = _attn(h, batch, attn_w[0], b_qkv[0], attn_sinks[0], attn_w[1], b_o[0], ln_mix_g[1], ln_mix_b[1],
                      ffn_weights(1))
    h, _ = _ffn(h, 1, p2d, *ffn1_w[:3], ln_ffn_g, ln_ffn_b, ffn1_w[3], b_ple_gate, ffn1_w[4], [])
    return h.reshape(batch, seq, d)
```

```python
import functools

import jax
import jax.numpy as jnp
from jax import lax
from jax.experimental import pallas as pl
from jax.experimental.pallas import tpu as pltpu

F32 = jnp.float32
BF16 = jnp.bfloat16

D_MODEL = 1024
DEPTH = 2
PLE_DIM = 256
SSD_HEADS = 16
SSD_HEAD_DIM = 64
SSD_INNER = SSD_HEADS * SSD_HEAD_DIM
SSD_GROUPS = 2
SSD_HPG = SSD_HEADS // SSD_GROUPS
SSD_STATE = 128
SSD_CONV = 4
SSD_CHUNK = 128
SSD_XBC = SSD_INNER + 2 * SSD_GROUPS * SSD_STATE
SC_DIM = 1024
SC_CONV = 3
MIX_OUT = SSD_INNER + SC_DIM
ATTN_HEADS = 16
ATTN_KV_HEADS = 4
ATTN_HEAD_DIM = 64
ATTN_BLOCK = 128
D_FF = ((8 * D_MODEL + 3 * 256 - 1) // (3 * 256)) * 256
ALPHA = (2 * DEPTH) ** 0.25
LN_EPS = 1e-5
LOG2E = 1.4426950408889634
RMS_EPS = 1e-5

LANES = 128
SUBLANES = 8
BF16_SUBLANES = 16
V7X_VMEM_BYTES = 64 * 1024 * 1024
VMEM_LIMIT_BYTES = V7X_VMEM_BYTES - 8 * 1024 * 1024

TM_MIX0 = 512
TM_ATTN = 1024
TM_FFN = 1024
FFN_ROWS = 256

_C_Z = 0
_C_XBC = _C_Z + SSD_INNER
_C_SCB = 0
_C_SCC = _C_SCB + SC_DIM
_C_SCH = _C_SCC + SC_DIM
_W_DT = SSD_INNER + SSD_XBC
_W_SC = _W_DT + SSD_HEADS
DT_REP = 2
COL_CHUNK = 256


def _dot(a, b):
    return jnp.dot(a, b, preferred_element_type=F32)


def _dot_nt(a, b):
    return lax.dot_general(a, b, (((1,), (1,)), ((), ())), preferred_element_type=F32)


def _dot_tn(a, b):
    return lax.dot_general(a, b, (((0,), (0,)), ((), ())), preferred_element_type=F32)


def _sigmoid(x):
    return 1.0 / (1.0 + jnp.exp(-x))


def _silu(x):
    return x * _sigmoid(x)


def _layer_norm(r, g, b):
    mu = jnp.mean(r, axis=-1, keepdims=True)
    xc = r - mu
    var = jnp.mean(xc * xc, axis=-1, keepdims=True)
    return xc * lax.rsqrt(var + LN_EPS) * g + b


def _split_bf16(v, parts):
    out = []
    rem = v
    for i in range(parts):
        p = rem.astype(BF16)
        out.append(p)
        if i + 1 < parts:
            rem = rem - p.astype(F32)
    return out


def _causal_conv(cur, prev_tail, w):
    width, cols = w.shape
    rows = lax.broadcasted_iota(jnp.int32, (SUBLANES, cols), 0)
    acc = cur * w[width - 1:width, :]
    for s in range(1, width):
        rolled = pltpu.roll(cur, s, 0)
        head = jnp.where(rows < s, pltpu.roll(prev_tail, s, 0), rolled[0:SUBLANES])
        shifted = jnp.concatenate([head, rolled[SUBLANES:]], axis=0)
        acc = acc + shifted * w[width - 1 - s:width - s, :]
    return acc


def _mixer0_kernel(x_ref, w_in_ref, cw_ref, cb_ref, dtb_ref, alog_ref, dexp_ref, nw_ref, scw_ref, w_out_ref,
                   e2_ref, tril_ref, o_ref,
                   xb_s, z_s, xbc_s, dt_s, acs_s, ycat_s, prev_xbc_s, prev_sc_s, h_s, w_sc_s, w_dt_s):
    tm = x_ref.shape[0]
    c = SSD_CHUNK
    gw = SSD_HPG * SSD_HEAD_DIM

    @pl.when(pl.program_id(1) == 0)
    def _():
        prev_xbc_s[...] = jnp.zeros_like(prev_xbc_s)
        prev_sc_s[...] = jnp.zeros_like(prev_sc_s)
        h_s[...] = jnp.zeros_like(h_s)

    @pl.when(jnp.logical_and(pl.program_id(0) == 0, pl.program_id(1) == 0))
    def _():
        for k in range(3 * SC_DIM // COL_CHUNK):
            cols = slice(k * COL_CHUNK, (k + 1) * COL_CHUNK)
            w_sc_s[:, cols] = w_in_ref[:, _W_SC + k * COL_CHUNK:_W_SC + (k + 1) * COL_CHUNK]
        blk = w_in_ref[:, _W_DT:_W_DT + LANES].astype(F32)
        lane = lax.broadcasted_iota(jnp.int32, blk.shape, 1)
        rep = jnp.where(lane < DT_REP * SSD_HEADS, pltpu.roll(blk, SSD_HEADS, 1), 0.0)
        w_dt_s[...] = jnp.where(lane < SSD_HEADS, blk, rep).astype(BF16)

    xb_s[...] = x_ref[...].astype(BF16)

    def proj(w_ref, lo, width):
        return _dot(xb_s[...], w_ref[:, lo:lo + width])

    def xbc_piece(k):
        cols = slice(k * COL_CHUNK, (k + 1) * COL_CHUNK)
        st = {}

        def mm():
            st["raw"] = proj(w_in_ref, _C_XBC + k * COL_CHUNK, COL_CHUNK)

        def vpu():
            raw = st.pop("raw")
            xbc_s[:, cols] = _silu(_causal_conv(raw, prev_xbc_s[:, cols], cw_ref[:, cols]) + cb_ref[:, cols])
            prev_xbc_s[:, cols] = raw[tm - SUBLANES:tm]

        return mm, vpu

    def dt_piece():
        st = {}

        def mm():
            st["raw"] = proj(w_dt_s, 0, LANES)

        def vpu():
            pre = st.pop("raw") + dtb_ref[...]
            dt = jnp.maximum(pre, 0.0) + jnp.log1p(jnp.exp(-jnp.abs(pre)))
            dt_s[...] = dt
            lane_row = lax.broadcasted_iota(jnp.int32, (1, LANES), 1)
            da = dt * jnp.where(lane_row < DT_REP * SSD_HEADS, -jnp.exp(alog_ref[...]), 0.0)
            da = jnp.concatenate([da[i * c:(i + 1) * c] for i in range(tm // c)], axis=1)
            acs_s[...] = _dot(tril_ref[...], jnp.concatenate(_split_bf16(da, 3), axis=0))

        return mm, vpu

    def z_piece(k):
        def mm():
            z_s[:, k * COL_CHUNK:(k + 1) * COL_CHUNK] = proj(w_in_ref, _C_Z + k * COL_CHUNK, COL_CHUNK)

        return mm

    def sc_piece(k):
        cols = slice(k * COL_CHUNK, (k + 1) * COL_CHUNK)
        st = {}

        def mm_b():
            st["b"] = proj(w_sc_s, _C_SCB + k * COL_CHUNK, COL_CHUNK)

        def mm_c():
            st["c"] = proj(w_sc_s, _C_SCC + k * COL_CHUNK, COL_CHUNK)

        def mm_h():
            st["u"] = st.pop("c") * proj(w_sc_s, _C_SCH + k * COL_CHUNK, COL_CHUNK)

        def vpu():
            u = st.pop("u")
            y_sc = st.pop("b") * _causal_conv(u, prev_sc_s[:, cols], scw_ref[:, cols])
            ycat_s[:, SSD_INNER + k * COL_CHUNK:SSD_INNER + (k + 1) * COL_CHUNK] = y_sc.astype(BF16)
            prev_sc_s[:, cols] = u[tm - SUBLANES:tm]

        return mm_b, mm_c, mm_h, vpu

    def out_piece(i):
        rows = slice(i * c, (i + 1) * c)
        st = {}

        def sc_half():
            st["mix"] = _dot(ycat_s[rows, SSD_INNER:], w_out_ref[SSD_INNER:, :])

        def ssd_half():
            mix = st.pop("mix") + _dot(ycat_s[rows, :SSD_INNER], w_out_ref[:SSD_INNER, :])
            o_ref[rows, :] = ALPHA * x_ref[rows, :] + mix

        return sc_half, ssd_half

    def ssd_chunk(ci):
        rows = slice(ci * c, (ci + 1) * c)
        st = {}

        def s1():
            lane_sq = lax.broadcasted_iota(jnp.int32, (c, LANES), 1)
            dt = dt_s[rows, :]
            a_cs = acs_s[:, ci * LANES:(ci + 1) * LANES]
            ecs = jnp.exp(a_cs)
            dtdte = dt * jnp.exp(a_cs[c - 1:c, :] - a_cs)
            q = jnp.where(lane_sq < SSD_HEADS, dtdte, ecs)
            ex = _dot(jnp.concatenate(_split_bf16(q, 2), axis=1), e2_ref[...])
            st["dtdte_e"] = ex[:, :SSD_INNER]
            st["ecs_e"] = ex[:, SSD_INNER:]
            st["a_cs"] = a_cs
            st["t"] = jnp.where(lane_sq < SSD_HEADS, a_cs, dt).T

        def s2():
            group_lane = lax.broadcasted_iota(jnp.int32, (c, SSD_GROUPS * SSD_STATE), 1) // SSD_STATE
            bmf = xbc_s[rows, SSD_INNER:SSD_INNER + SSD_GROUPS * SSD_STATE]
            bm = bmf.astype(BF16)
            cm = xbc_s[rows, SSD_INNER + SSD_GROUPS * SSD_STATE:].astype(BF16)
            st["bm"] = bm
            st["cm"] = cm
            b_diag = jnp.concatenate([jnp.where(group_lane == g, bmf, 0.0) for g in range(SSD_GROUPS)], axis=0)
            st["cb"] = _dot_nt(cm, b_diag.astype(BF16))
            st["y"] = []

        def s2_quad(qi):
            lane_sq = lax.broadcasted_iota(jnp.int32, (c, LANES), 1)
            causal = lax.broadcasted_iota(jnp.int32, (c, LANES), 0) >= lane_sq
            quad_lane = lax.broadcasted_iota(jnp.int32, (c, 4 * SSD_HEAD_DIM), 1) // SSD_HEAD_DIM
            g = (4 * qi) // SSD_HPG
            h0 = 4 * qi
            a_cs = st["a_cs"]
            t = st["t"]
            cb = st["cb"][:, g * c:(g + 1) * c]
            ms = []
            for h in range(h0, h0 + 4):
                seg = jnp.broadcast_to(a_cs[:, h:h + 1], (c, LANES)) - t[h:h + 1, :]
                dec = jnp.exp(jnp.where(causal, seg, -jnp.inf))
                ms.append((cb * dec * t[SSD_HEADS + h:SSD_HEADS + h + 1, :]).astype(BF16))
            xq = xbc_s[rows, h0 * SSD_HEAD_DIM:(h0 + 4) * SSD_HEAD_DIM]
            rhs = jnp.concatenate([jnp.where(quad_lane == k, xq, 0.0) for k in range(4)], axis=0)
            st["y"].append(_dot(jnp.concatenate(ms, axis=1), rhs.astype(BF16)))

        def s3():
            for key in ("a_cs", "t", "cb"):
                st.pop(key)
            bm = st.pop("bm")
            cm = st.pop("cm")
            ecs_e = st["ecs_e"]
            dtdte_e = st.pop("dtdte_e")
            y_off = []
            for g in range(SSD_GROUPS):
                hg = h_s[g]
                y_off.append(_dot(cm[:, g * SSD_STATE:(g + 1) * SSD_STATE], hg.astype(BF16)))
                xd = (xbc_s[rows, g * gw:(g + 1) * gw] * dtdte_e[:, g * gw:(g + 1) * gw]).astype(BF16)
                new = _dot_tn(bm[:, g * SSD_STATE:(g + 1) * SSD_STATE], xd)
                h_s[g] = hg * ecs_e[c - 1:c, g * gw:(g + 1) * gw] + new
            st["y_off"] = jnp.concatenate(y_off, axis=1)

        def s4():
            y = jnp.concatenate(st.pop("y"), axis=1) + st.pop("y_off") * st.pop("ecs_e")
            y = y + xbc_s[rows, :SSD_INNER] * dexp_ref[...]
            y = y * _silu(z_s[rows, :])
            yn = []
            for g in range(SSD_GROUPS):
                yg = y[:, g * gw:(g + 1) * gw]
                yn.append(yg * lax.rsqrt(jnp.mean(yg * yg, axis=-1, keepdims=True) + RMS_EPS))
            ycat_s[rows, 0:SSD_INNER] = (jnp.concatenate(yn, axis=1) * nw_ref[...]).astype(BF16)

        quads = [functools.partial(s2_quad, qi) for qi in range(SSD_HEADS // 4)]
        return [s1, s2] + quads + [s3, s4]

    head = [xbc_piece(k) for k in range(SSD_XBC // COL_CHUNK)] + [dt_piece()]
    z_tasks = [z_piece(k) for k in range(SSD_INNER // COL_CHUNK)]
    pending = None
    for mm, vpu in head:
        mm()
        if pending is not None:
            if z_tasks:
                z_tasks.pop(0)()
            pending()
        pending = vpu
    pending()

    n_chunks = tm // c
    fillers = [(0, t) for t in z_tasks]
    for k in range(SC_DIM // COL_CHUNK):
        fillers.extend((0, t) for t in sc_piece(k))
    outs = [out_piece(i) for i in range(n_chunks)]
    fillers.extend((0, sc_half) for sc_half, _ in outs)
    per_chunk = len(ssd_chunk(0))
    fillers.extend((per_chunk * (i + 1), ssd_half) for i, (_, ssd_half) in enumerate(outs))
    stages = [s for ci in range(n_chunks) for s in ssd_chunk(ci)]
    n_fill = len(fillers)
    for i, stage in enumerate(stages):
        want = ((i + 1) * n_fill) // len(stages) - (n_fill - len(fillers))
        while want > 0 and fillers and fillers[0][0] <= i:
            fillers.pop(0)[1]()
            want -= 1
        stage()
    for _, task in fillers:
        task()


def _attn_kernel(sink_ref, x_ref, w_qkv_ref, b_qkv_ref, w_o_ref, b_o_ref, ones_ref, o_ref,
                 xb_s, q_s, kt_s, v_s, a_s, kprev_s, vprev_s):
    tm = x_ref.shape[0]
    blk = ATTN_BLOCK
    first_tile = pl.program_id(1) == 0
    nq = ATTN_HEADS * ATTN_HEAD_DIM
    nkv = ATTN_KV_HEADS * ATTN_HEAD_DIM
    nkv2 = ATTN_KV_HEADS * LANES
    half = ATTN_HEAD_DIM

    @pl.when(first_tile)
    def _():
        kprev_s[...] = jnp.zeros_like(kprev_s)
        vprev_s[...] = jnp.zeros_like(vprev_s)

    kt_s[:, :blk] = kprev_s[...]
    v_s[:blk, :] = vprev_s[...]
    xb_s[...] = x_ref[...].astype(BF16)

    def proj_piece(k):
        lo = k * COL_CHUNK
        st = {}

        def mm():
            st["v"] = _dot(xb_s[...], w_qkv_ref[:, lo:lo + COL_CHUNK]) + b_qkv_ref[:, lo:lo + COL_CHUNK]

        def q_vpu():
            q_s[:, lo:lo + COL_CHUNK] = (st.pop("v") * (ATTN_HEAD_DIM ** -0.5 * LOG2E)).astype(BF16)

        def k_vpu():
            kt = st.pop("v").T.astype(BF16)
            zero = jnp.zeros((half, tm), BF16)
            for h in range(ATTN_KV_HEADS):
                kh = kt[h * half:(h + 1) * half]
                for base, part in ((0, jnp.concatenate([kh, zero], axis=0)),
                                   (nkv2, jnp.concatenate([zero, kh], axis=0))):
                    kt_s[base + h * LANES:base + (h + 1) * LANES, blk:] = part
                    kprev_s[base + h * LANES:base + (h + 1) * LANES, :] = part[:, tm - blk:]

        def v_vpu():
            vd = st.pop("v")
            lo_c = lax.broadcasted_iota(jnp.int32, (tm, LANES), 1) < half
            for c2 in range(ATTN_KV_HEADS // 2):
                col = vd[:, c2 * LANES:(c2 + 1) * LANES]
                swapped = pltpu.roll(col, half, 1)
                planes = ((0, 2 * c2, jnp.where(lo_c, col, 0.0)), (nkv2, 2 * c2, jnp.where(lo_c, 0.0, swapped)),
                          (0, 2 * c2 + 1, jnp.where(lo_c, swapped, 0.0)), (nkv2, 2 * c2 + 1, jnp.where(lo_c, 0.0, col)))
                for base, h, part in planes:
                    part = part.astype(BF16)
                    v_s[blk:, base + h * LANES:base + (h + 1) * LANES] = part
                    vprev_s[:, base + h * LANES:base + (h + 1) * LANES] = part[tm - blk:]

        vpu = q_vpu if lo < nq else (k_vpu if lo < nq + nkv else v_vpu)
        return mm, vpu

    pending = None
    for k in range((nq + 2 * nkv) // COL_CHUNK):
        mm, vpu = proj_piece(k)
        mm()
        if pending is not None:
            pending()
        pending = vpu
    pending()

    qi = lax.broadcasted_iota(jnp.int32, (blk, 2 * blk), 0)
    kj = lax.broadcasted_iota(jnp.int32, (blk, 2 * blk), 1)
    band = (kj > qi) & (kj <= qi + blk)
    band0 = band & (kj >= jnp.where(first_tile, blk, 0))
    lane_o = lax.broadcasted_iota(jnp.int32, (blk, LANES), 1) < half

    def scores(n, j):
        c0 = (j // 2) * LANES
        keys = slice(n * blk, (n + 2) * blk)
        kk = jnp.concatenate([kt_s[c0:c0 + LANES, keys], kt_s[nkv2 + c0:nkv2 + c0 + LANES, keys]], axis=1)
        return _dot(q_s[n * blk:(n + 1) * blk, j * LANES:(j + 1) * LANES], kk)

    def probs(n, j, s):
        valid = band0 if n == 0 else band
        ps = []
        es = []
        for t in range(2):
            sink = sink_ref[2 * j + t] * LOG2E
            sh = jnp.where(valid, s[:, t * 2 * blk:(t + 1) * 2 * blk], -jnp.inf)
            m = jnp.maximum(jnp.max(sh, axis=-1, keepdims=True), sink)
            ps.append(jnp.exp2(sh - m).astype(BF16))
            es.append(jnp.exp2(sink - m))
        return jnp.concatenate(ps, axis=1), jnp.where(lane_o, es[0], es[1])

    def output(n, j, p, es):
        c0 = (j // 2) * LANES
        keys = slice(n * blk, (n + 2) * blk)
        vv = jnp.concatenate([v_s[keys, c0:c0 + LANES], v_s[keys, nkv2 + c0:nkv2 + c0 + LANES]], axis=0)
        r = _dot(p, jnp.concatenate([vv, ones_ref[...]], axis=1))
        a_s[n * blk:(n + 1) * blk, j * LANES:(j + 1) * LANES] = (r[:, :LANES] / (r[:, LANES:] + es)).astype(BF16)

    def out_block(n):
        rows = slice(n * blk, (n + 1) * blk)
        st = {}

        def mm():
            st["mix"] = _dot(a_s[rows, :], w_o_ref[...]) + b_o_ref[...]

        def residual():
            o_ref[rows, :] = ALPHA * x_ref[rows, :] + st.pop("mix")

        return mm, residual

    steps = [(n, j) for n in range(tm // blk) for j in range(ATTN_HEADS // 2)]
    pairs = ATTN_HEADS // 2
    extra = {}
    for n in range(tm // blk - 1):
        mm, norm = out_block(n)
        extra[pairs * (n + 1) + 3] = mm
        extra[pairs * (n + 1) + 6] = norm
    s_val = {}
    p_val = {}
    for i in range(len(steps) + 2):
        if i < len(steps):
            s_val[i] = scores(*steps[i])
        if i in extra:
            extra[i]()
        if 0 <= i - 1 < len(steps):
            p_val[i - 1] = probs(*steps[i - 1], s_val.pop(i - 1))
        if 0 <= i - 2 < len(steps):
            output(*steps[i - 2], *p_val.pop(i - 2))
    for task in out_block(tm // blk - 1):
        task()


def _ffn_kernel(r_ref, p_ref, gm_ref, bm_ref, wg_ref, wu_ref, wd_ref, g_ref, b_ref, wpg_ref, bpg_ref, wple_ref, o_ref,
                x1_s):
    tm = r_ref.shape[0]
    n_blk = tm // FFN_ROWS

    def blk(i):
        rows = slice(i * FFN_ROWS, (i + 1) * FFN_ROWS)
        st = {}

        def mixer_norm():
            x1 = _layer_norm(r_ref[rows, :], gm_ref[...], bm_ref[...])
            x1_s[rows, :] = x1
            st["xb"] = x1.astype(BF16)

        def gate_up():
            xb = st.pop("xb")
            st["g"] = _dot(xb, wg_ref[...])
            st["u"] = _dot(xb, wu_ref[...])

        def act():
            st["h"] = (_silu(st.pop("g")) * st.pop("u")).astype(BF16)

        def down():
            st["y"] = _dot(st.pop("h"), wd_ref[...])

        def norm():
            st["x2"] = _layer_norm(ALPHA * x1_s[rows, :] + st.pop("y"), g_ref[...], b_ref[...])

        def ple():
            st["gate"] = _dot(st["x2"].astype(BF16), wpg_ref[...])
            st["pe"] = _dot(p_ref[rows, :].astype(BF16), wple_ref[...])

        def out():
            o_ref[rows, :] = st.pop("x2") + st.pop("pe") * _sigmoid(st.pop("gate") + bpg_ref[...])

        return [mixer_norm, gate_up, act, down, norm, ple, out]

    stages = [blk(i) for i in range(n_blk)]
    n_st = len(stages[0])
    for step in range(n_st + 2 * (n_blk - 1)):
        for i in range(n_blk):
            k = step - 2 * i
            if 0 <= k < n_st:
                stages[i][k]()


def _with_casts(body, n_in, n_cast):
    def kernel_fn(*refs):
        ins, casts_in = refs[:n_in], refs[n_in:n_in + n_cast]
        out, casts_out = refs[n_in + n_cast], refs[n_in + n_cast + 1:n_in + 2 * n_cast + 1]
        scratch = refs[n_in + 2 * n_cast + 1:]
        body(*ins, out, *scratch)
        for src, dst in zip(casts_in, casts_out):
            dst[...] = src[...].astype(BF16)

    return kernel_fn


def _cast_plan(w, layer, steps, flat_step):
    _, rows, cols = w.shape
    per, reuse = rows // steps, 1
    while per % BF16_SUBLANES:
        per, reuse = per * 2, reuse * 2
    assert rows % per == 0
    in_spec = pl.BlockSpec((None, per, cols), lambda *g: (layer, flat_step(*g) // reuse, 0))
    out_spec = pl.BlockSpec((per, cols), lambda *g: (flat_step(*g) // reuse, 0))
    return in_spec, out_spec, jax.ShapeDtypeStruct((rows, cols), BF16)


def _resident(shape):
    return pl.BlockSpec(shape, lambda *_: (0,) * len(shape), pipeline_mode=pl.Buffered(1))


def _row(v):
    return v.reshape(1, -1).astype(F32)


def _pad_lanes(v, width):
    return jnp.pad(v, ((0, 0), (0, width - v.shape[1])))


def _mixer0(x2d, batch, w_in, conv_w, conv_b, dt_bias, a_log, d_skip, norm_w, sc_conv_w, w_out, next_w):
    t = x2d.shape[0]
    n_l = t // batch // TM_MIX0
    w_in = w_in.astype(BF16)
    dtb = _pad_lanes(jnp.tile(_row(dt_bias), (1, DT_REP)), LANES)
    alog = _pad_lanes(jnp.tile(_row(a_log), (1, DT_REP)), LANES)
    dexp = jnp.repeat(_row(d_skip), SSD_HEAD_DIM, axis=1)
    r = jnp.arange(LANES)[:, None]
    col = jnp.arange(DT_REP * SSD_INNER)[None, :]
    e1 = ((r < DT_REP * SSD_HEADS) & (col // SSD_HEAD_DIM == r)).astype(BF16)
    e2 = jnp.concatenate([e1, e1], axis=0)
    tri = (jnp.arange(SSD_CHUNK)[:, None] >= jnp.arange(SSD_CHUNK)[None, :]).astype(BF16)
    tril3 = jnp.concatenate([tri, tri, tri], axis=1)

    tok = lambda width: pl.BlockSpec((TM_MIX0, width), lambda b, l: (b * n_l + l, 0))
    plans = [_cast_plan(w, layer, batch * n_l, lambda b, l: b * n_l + l) for w, layer in next_w]
    in_specs = [tok(D_MODEL), _resident(w_in.shape), _resident((SSD_CONV, SSD_XBC)),
                _resident((1, SSD_XBC)), _resident((1, LANES)), _resident((1, LANES)),
                _resident((1, SSD_INNER)), _resident((1, SSD_INNER)), _resident((SC_CONV, SC_DIM)),
                _resident((MIX_OUT, D_MODEL)), _resident(e2.shape), _resident(tril3.shape)]
    outs = pl.pallas_call(
        _with_casts(_mixer0_kernel, len(in_specs), len(plans)),
        grid=(batch, n_l),
        in_specs=in_specs + [p[0] for p in plans],
        out_specs=[tok(D_MODEL)] + [p[1] for p in plans],
        out_shape=[jax.ShapeDtypeStruct((t, D_MODEL), F32)] + [p[2] for p in plans],
        scratch_shapes=[pltpu.VMEM((TM_MIX0, D_MODEL), BF16),
                        pltpu.VMEM((TM_MIX0, SSD_INNER), F32), pltpu.VMEM((TM_MIX0, SSD_XBC), F32),
                        pltpu.VMEM((TM_MIX0, LANES), F32), pltpu.VMEM((SSD_CHUNK, TM_MIX0), F32),
                        pltpu.VMEM((TM_MIX0, MIX_OUT), BF16),
                        pltpu.VMEM((SUBLANES, SSD_XBC), F32), pltpu.VMEM((SUBLANES, SC_DIM), F32),
                        pltpu.VMEM((SSD_GROUPS, SSD_STATE, SSD_HPG * SSD_HEAD_DIM), F32),
                        pltpu.VMEM((D_MODEL, 3 * SC_DIM), BF16), pltpu.VMEM((D_MODEL, LANES), BF16)],
        compiler_params=pltpu.CompilerParams(dimension_semantics=("arbitrary", "arbitrary"),
                                             vmem_limit_bytes=VMEM_LIMIT_BYTES),
        name="mixer0_ssd_shortconv",
    )(x2d, w_in, conv_w.astype(F32), _row(conv_b), dtb, alog, dexp, _row(norm_w), sc_conv_w.astype(F32),
      w_out.astype(BF16), e2, tril3, *[w for w, _ in next_w])
    return outs[0], outs[1:]


def _attn(x2d, batch, w_qkv, b_qkv, sinks, w_o, b_o, next_w):
    t = x2d.shape[0]
    n_l = t // batch // TM_ATTN
    nq = ATTN_HEADS * ATTN_HEAD_DIM
    nkv = ATTN_KV_HEADS * ATTN_HEAD_DIM

    width = nq + 2 * nkv
    ones_blk = (jnp.arange(4 * ATTN_BLOCK)[:, None] // (2 * ATTN_BLOCK) == jnp.arange(LANES)[None, :] // ATTN_HEAD_DIM
                ).astype(BF16)
    tok = pl.BlockSpec((TM_ATTN, D_MODEL), lambda b, l: (b * n_l + l, 0))
    plans = [_cast_plan(w, layer, batch * n_l, lambda b, l: b * n_l + l) for w, layer in next_w]
    in_specs = [pl.BlockSpec(memory_space=pltpu.SMEM), tok, _resident((D_MODEL, width)), _resident((1, width)),
                _resident((nq, D_MODEL)), _resident((1, D_MODEL)), _resident(ones_blk.shape)]
    outs = pl.pallas_call(
        _with_casts(_attn_kernel, len(in_specs), len(plans)),
        grid=(batch, n_l),
        in_specs=in_specs + [p[0] for p in plans],
        out_specs=[tok] + [p[1] for p in plans],
        out_shape=[jax.ShapeDtypeStruct((t, D_MODEL), F32)] + [p[2] for p in plans],
        scratch_shapes=[pltpu.VMEM((TM_ATTN, D_MODEL), BF16),
                        pltpu.VMEM((TM_ATTN, nq), BF16), pltpu.VMEM((4 * nkv, TM_ATTN + ATTN_BLOCK), BF16),
                        pltpu.VMEM((TM_ATTN + ATTN_BLOCK, 4 * nkv), BF16), pltpu.VMEM((TM_ATTN, nq), BF16),
                        pltpu.VMEM((4 * nkv, ATTN_BLOCK), BF16), pltpu.VMEM((ATTN_BLOCK, 4 * nkv), BF16)],
        compiler_params=pltpu.CompilerParams(dimension_semantics=("arbitrary", "arbitrary"),
                                             vmem_limit_bytes=VMEM_LIMIT_BYTES),
        name="mixer1_swa",
    )(sinks.astype(F32), x2d, w_qkv, _row(b_qkv), w_o, _row(b_o), ones_blk,
      *[w for w, _ in next_w])
    return outs[0], outs[1:]


def _ffn(r2d, layer, p3d, ln_mix_g, ln_mix_b, w_gate, w_up, w_down, ln_g, ln_b, w_ple_gate, b_ple_gate, w_ple, next_w):
    t = r2d.shape[0]
    steps = t // TM_FFN
    tok = lambda width: pl.BlockSpec((TM_FFN, width), lambda i: (i, 0))
    plans = [_cast_plan(w, lyr, steps, lambda i: i) for w, lyr in next_w]
    in_specs = [tok(D_MODEL), pl.BlockSpec((None, TM_FFN, PLE_DIM), lambda i: (layer, i, 0)),
                _resident((1, D_MODEL)), _resident((1, D_MODEL)),
                _resident((D_MODEL, D_FF)), _resident((D_MODEL, D_FF)), _resident((D_FF, D_MODEL)),
                _resident((1, D_MODEL)), _resident((1, D_MODEL)),
                _resident((D_MODEL, D_MODEL)), _resident((1, D_MODEL)), _resident((PLE_DIM, D_MODEL))]
    outs = pl.pallas_call(
        _with_casts(_ffn_kernel, len(in_specs), len(plans)),
        grid=(steps,),
        in_specs=in_specs + [p[0] for p in plans],
        out_specs=[tok(D_MODEL)] + [p[1] for p in plans],
        out_shape=[jax.ShapeDtypeStruct((t, D_MODEL), F32)] + [p[2] for p in plans],
        scratch_shapes=[pltpu.VMEM((TM_FFN, D_MODEL), F32)],
        compiler_params=pltpu.CompilerParams(dimension_semantics=("arbitrary",),
                                             vmem_limit_bytes=VMEM_LIMIT_BYTES),
        name="ffn_ple",
    )(r2d, p3d, _row(ln_mix_g[layer]), _row(ln_mix_b[layer]), w_gate, w_up, w_down, _row(ln_g[layer]), _row(ln_b[layer]),
      w_ple_gate, _row(b_ple_gate[layer]), w_ple, *[w for w, _ in next_w])
    return outs[0], outs[1:]


def kernel(x, p, w_in_mix, ssd_conv_w, ssd_conv_b, ssd_dt_bias, ssd_a_log, ssd_d, ssd_norm_w, sc_conv_w, w_out_mix, w_qkv, b_qkv, attn_sinks, w_o, b_o, ln_mix_g, ln_mix_b, w_ffn_gate, w_ffn_up, w_ffn_down, ln_ffn_g, ln_ffn_b, w_ple, w_ple_gate, b_ple_gate):
    batch, seq, d = x.shape
    assert d == D_MODEL and seq % TM_MIX0 == 0 and seq % TM_ATTN == 0 and (batch * seq) % TM_FFN == 0
    assert p.shape == (DEPTH, batch, seq, PLE_DIM)
    h = x.reshape(batch * seq, d)
    p2d = p.reshape(DEPTH, batch * seq, PLE_DIM)

    def ffn_weights(i):
        return [(w, i) for w in (w_ffn_gate, w_ffn_up, w_ffn_down, w_ple_gate, w_ple)]

    r, ffn0_w = _mixer0(h, batch, w_in_mix[0], ssd_conv_w[0], ssd_conv_b[0], ssd_dt_bias[0], ssd_a_log[0], ssd_d[0],
                        ssd_norm_w[0], sc_conv_w[0], w_out_mix[0], ffn_weights(0))
    h, attn_w = _ffn(r, 0, p2d, ln_mix_g, ln_mix_b, *ffn0_w[:3], ln_ffn_g, ln_ffn_b, ffn0_w[3], b_ple_gate, ffn0_w[4],
                     [(w_qkv, 0), (w_o, 0)])
    r, ffn1_w = _attn(h, batch, attn_w[0], b_qkv[0], attn_sinks[0], attn_w[1], b_o[0], ffn_weights(1))
    h, _ = _ffn(r, 1, p2d, ln_mix_g, ln_mix_b, *ffn1_w[:3], ln_ffn_g, ln_ffn_b, ffn1_w[3], b_ple_gate, ffn1_w[4], [])
    return h.reshape(batch, seq, d)
```

```python
import functools

import jax
import jax.numpy as jnp
from jax import lax
from jax.experimental import pallas as pl
from jax.experimental.pallas import tpu as pltpu

F32 = jnp.float32
BF16 = jnp.bfloat16

D_MODEL = 1024
DEPTH = 2
PLE_DIM = 256
SSD_HEADS = 16
SSD_HEAD_DIM = 64
SSD_INNER = SSD_HEADS * SSD_HEAD_DIM
SSD_GROUPS = 2
SSD_HPG = SSD_HEADS // SSD_GROUPS
SSD_STATE = 128
SSD_CONV = 4
SSD_CHUNK = 128
SSD_XBC = SSD_INNER + 2 * SSD_GROUPS * SSD_STATE
SC_DIM = 1024
SC_CONV = 3
MIX_OUT = SSD_INNER + SC_DIM
ATTN_HEADS = 16
ATTN_KV_HEADS = 4
ATTN_HEAD_DIM = 64
ATTN_BLOCK = 128
D_FF = ((8 * D_MODEL + 3 * 256 - 1) // (3 * 256)) * 256
ALPHA = (2 * DEPTH) ** 0.25
LN_EPS = 1e-5
LOG2E = 1.4426950408889634
RMS_EPS = 1e-5

LANES = 128
SUBLANES = 8
BF16_SUBLANES = 16
V7X_VMEM_BYTES = 64 * 1024 * 1024
VMEM_LIMIT_BYTES = V7X_VMEM_BYTES - 8 * 1024 * 1024

TM_MIX0 = 512
TM_ATTN = 1024
TM_FFN = 1024
FFN_ROWS = 256

_C_Z = 0
_C_XBC = _C_Z + SSD_INNER
_C_SCB = 0
_C_SCC = _C_SCB + SC_DIM
_C_SCH = _C_SCC + SC_DIM
_W_DT = SSD_INNER + SSD_XBC
_W_SC = _W_DT + SSD_HEADS
DT_REP = 2
COL_CHUNK = 256


def _dot(a, b):
    return jnp.dot(a, b, preferred_element_type=F32)


def _dot_nt(a, b):
    return lax.dot_general(a, b, (((1,), (1,)), ((), ())), preferred_element_type=F32)


def _dot_tn(a, b):
    return lax.dot_general(a, b, (((0,), (0,)), ((), ())), preferred_element_type=F32)


def _sigmoid(x):
    return 1.0 / (1.0 + jnp.exp(-x))


def _silu(x):
    return x * _sigmoid(x)


def _layer_norm(r, g, b):
    mu = jnp.mean(r, axis=-1, keepdims=True)
    xc = r - mu
    var = jnp.mean(xc * xc, axis=-1, keepdims=True)
    return xc * lax.rsqrt(var + LN_EPS) * g + b


def _split_bf16(v, parts):
    out = []
    rem = v
    for i in range(parts):
        p = rem.astype(BF16)
        out.append(p)
        if i + 1 < parts:
            rem = rem - p.astype(F32)
    return out


def _causal_conv(cur, prev_tail, w):
    width, cols = w.shape
    rows = lax.broadcasted_iota(jnp.int32, (SUBLANES, cols), 0)
    acc = cur * w[width - 1:width, :]
    for s in range(1, width):
        rolled = pltpu.roll(cur, s, 0)
        head = jnp.where(rows < s, pltpu.roll(prev_tail, s, 0), rolled[0:SUBLANES])
        shifted = jnp.concatenate([head, rolled[SUBLANES:]], axis=0)
        acc = acc + shifted * w[width - 1 - s:width - s, :]
    return acc


def _mixer0_kernel(x_ref, w_in_ref, cw_ref, cb_ref, dtb_ref, alog_ref, dexp_ref, nw_ref, scw_ref, w_out_ref,
                   g_ref, b_ref, e2_ref, tril_ref, o_ref,
                   xb_s, z_s, xbc_s, dt_s, acs_s, ycat_s, prev_xbc_s, prev_sc_s, h_s, w_sc_s, w_dt_s):
    tm = x_ref.shape[0]
    c = SSD_CHUNK
    gw = SSD_HPG * SSD_HEAD_DIM

    @pl.when(pl.program_id(1) == 0)
    def _():
        prev_xbc_s[...] = jnp.zeros_like(prev_xbc_s)
        prev_sc_s[...] = jnp.zeros_like(prev_sc_s)
        h_s[...] = jnp.zeros_like(h_s)

    @pl.when(jnp.logical_and(pl.program_id(0) == 0, pl.program_id(1) == 0))
    def _():
        for k in range(3 * SC_DIM // COL_CHUNK):
            cols = slice(k * COL_CHUNK, (k + 1) * COL_CHUNK)
            w_sc_s[:, cols] = w_in_ref[:, _W_SC + k * COL_CHUNK:_W_SC + (k + 1) * COL_CHUNK]
        blk = w_in_ref[:, _W_DT:_W_DT + LANES].astype(F32)
        lane = lax.broadcasted_iota(jnp.int32, blk.shape, 1)
        rep = jnp.where(lane < DT_REP * SSD_HEADS, pltpu.roll(blk, SSD_HEADS, 1), 0.0)
        w_dt_s[...] = jnp.where(lane < SSD_HEADS, blk, rep).astype(BF16)

    xb_s[...] = x_ref[...].astype(BF16)

    def proj(w_ref, lo, width):
        return _dot(xb_s[...], w_ref[:, lo:lo + width])

    def xbc_piece(k):
        cols = slice(k * COL_CHUNK, (k + 1) * COL_CHUNK)
        st = {}

        def mm():
            st["raw"] = proj(w_in_ref, _C_XBC + k * COL_CHUNK, COL_CHUNK)

        def vpu():
            raw = st.pop("raw")
            xbc_s[:, cols] = _silu(_causal_conv(raw, prev_xbc_s[:, cols], cw_ref[:, cols]) + cb_ref[:, cols])
            prev_xbc_s[:, cols] = raw[tm - SUBLANES:tm]

        return mm, vpu

    def dt_piece():
        st = {}

        def mm():
            st["raw"] = proj(w_dt_s, 0, LANES)

        def vpu():
            pre = st.pop("raw") + dtb_ref[...]
            dt = jnp.maximum(pre, 0.0) + jnp.log1p(jnp.exp(-jnp.abs(pre)))
            dt_s[...] = dt
            lane_row = lax.broadcasted_iota(jnp.int32, (1, LANES), 1)
            da = dt * jnp.where(lane_row < DT_REP * SSD_HEADS, -jnp.exp(alog_ref[...]), 0.0)
            da = jnp.concatenate([da[i * c:(i + 1) * c] for i in range(tm // c)], axis=1)
            acs_s[...] = _dot(tril_ref[...], jnp.concatenate(_split_bf16(da, 3), axis=0))

        return mm, vpu

    def z_piece(k):
        def mm():
            z_s[:, k * COL_CHUNK:(k + 1) * COL_CHUNK] = proj(w_in_ref, _C_Z + k * COL_CHUNK, COL_CHUNK)

        return mm

    def sc_piece(k):
        cols = slice(k * COL_CHUNK, (k + 1) * COL_CHUNK)
        st = {}

        def mm_b():
            st["b"] = proj(w_sc_s, _C_SCB + k * COL_CHUNK, COL_CHUNK)

        def mm_c():
            st["c"] = proj(w_sc_s, _C_SCC + k * COL_CHUNK, COL_CHUNK)

        def mm_h():
            st["u"] = st.pop("c") * proj(w_sc_s, _C_SCH + k * COL_CHUNK, COL_CHUNK)

        def vpu():
            u = st.pop("u")
            y_sc = st.pop("b") * _causal_conv(u, prev_sc_s[:, cols], scw_ref[:, cols])
            ycat_s[:, SSD_INNER + k * COL_CHUNK:SSD_INNER + (k + 1) * COL_CHUNK] = y_sc.astype(BF16)
            prev_sc_s[:, cols] = u[tm - SUBLANES:tm]

        return mm_b, mm_c, mm_h, vpu

    def out_piece(i):
        rows = slice(i * c, (i + 1) * c)
        st = {}

        def sc_half():
            st["mix"] = _dot(ycat_s[rows, SSD_INNER:], w_out_ref[SSD_INNER:, :])

        def ssd_half():
            mix = st.pop("mix") + _dot(ycat_s[rows, :SSD_INNER], w_out_ref[:SSD_INNER, :])
            o_ref[rows, :] = _layer_norm(ALPHA * x_ref[rows, :] + mix, g_ref[...], b_ref[...])

        return sc_half, ssd_half

    def ssd_chunk(ci):
        rows = slice(ci * c, (ci + 1) * c)
        st = {}

        def s1():
            lane_sq = lax.broadcasted_iota(jnp.int32, (c, LANES), 1)
            dt = dt_s[rows, :]
            a_cs = acs_s[:, ci * LANES:(ci + 1) * LANES]
            ecs = jnp.exp(a_cs)
            dtdte = dt * jnp.exp(a_cs[c - 1:c, :] - a_cs)
            q = jnp.where(lane_sq < SSD_HEADS, dtdte, ecs)
            ex = _dot(jnp.concatenate(_split_bf16(q, 2), axis=1), e2_ref[...])
            st["dtdte_e"] = ex[:, :SSD_INNER]
            st["ecs_e"] = ex[:, SSD_INNER:]
            st["a_cs"] = a_cs
            st["t"] = jnp.where(lane_sq < SSD_HEADS, a_cs, dt).T

        def s2():
            group_lane = lax.broadcasted_iota(jnp.int32, (c, SSD_GROUPS * SSD_STATE), 1) // SSD_STATE
            bmf = xbc_s[rows, SSD_INNER:SSD_INNER + SSD_GROUPS * SSD_STATE]
            bm = bmf.astype(BF16)
            cm = xbc_s[rows, SSD_INNER + SSD_GROUPS * SSD_STATE:].astype(BF16)
            st["bm"] = bm
            st["cm"] = cm
            b_diag = jnp.concatenate([jnp.where(group_lane == g, bmf, 0.0) for g in range(SSD_GROUPS)], axis=0)
            st["cb"] = _dot_nt(cm, b_diag.astype(BF16))
            st["y"] = []

        def s2_quad(qi):
            lane_sq = lax.broadcasted_iota(jnp.int32, (c, LANES), 1)
            causal = lax.broadcasted_iota(jnp.int32, (c, LANES), 0) >= lane_sq
            quad_lane = lax.broadcasted_iota(jnp.int32, (c, 4 * SSD_HEAD_DIM), 1) // SSD_HEAD_DIM
            g = (4 * qi) // SSD_HPG
            h0 = 4 * qi
            a_cs = st["a_cs"]
            t = st["t"]
            cb = st["cb"][:, g * c:(g + 1) * c]
            ms = []
            for h in range(h0, h0 + 4):
                seg = jnp.broadcast_to(a_cs[:, h:h + 1], (c, LANES)) - t[h:h + 1, :]
                dec = jnp.exp(jnp.where(causal, seg, -jnp.inf))
                ms.append((cb * dec * t[SSD_HEADS + h:SSD_HEADS + h + 1, :]).astype(BF16))
            xq = xbc_s[rows, h0 * SSD_HEAD_DIM:(h0 + 4) * SSD_HEAD_DIM]
            rhs = jnp.concatenate([jnp.where(quad_lane == k, xq, 0.0) for k in range(4)], axis=0)
            st["y"].append(_dot(jnp.concatenate(ms, axis=1), rhs.astype(BF16)))

        def s3():
            for key in ("a_cs", "t", "cb"):
                st.pop(key)
            bm = st.pop("bm")
            cm = st.pop("cm")
            ecs_e = st["ecs_e"]
            dtdte_e = st.pop("dtdte_e")
            y_off = []
            for g in range(SSD_GROUPS):
                hg = h_s[g]
                y_off.append(_dot(cm[:, g * SSD_STATE:(g + 1) * SSD_STATE], hg.astype(BF16)))
                xd = (xbc_s[rows, g * gw:(g + 1) * gw] * dtdte_e[:, g * gw:(g + 1) * gw]).astype(BF16)
                new = _dot_tn(bm[:, g * SSD_STATE:(g + 1) * SSD_STATE], xd)
                h_s[g] = hg * ecs_e[c - 1:c, g * gw:(g + 1) * gw] + new
            st["y_off"] = jnp.concatenate(y_off, axis=1)

        def s4():
            y = jnp.concatenate(st.pop("y"), axis=1) + st.pop("y_off") * st.pop("ecs_e")
            y = y + xbc_s[rows, :SSD_INNER] * dexp_ref[...]
            y = y * _silu(z_s[rows, :])
            yn = []
            for g in range(SSD_GROUPS):
                yg = y[:, g * gw:(g + 1) * gw]
                yn.append(yg * lax.rsqrt(jnp.mean(yg * yg, axis=-1, keepdims=True) + RMS_EPS))
            ycat_s[rows, 0:SSD_INNER] = (jnp.concatenate(yn, axis=1) * nw_ref[...]).astype(BF16)

        quads = [functools.partial(s2_quad, qi) for qi in range(SSD_HEADS // 4)]
        return [s1, s2] + quads + [s3, s4]

    head = [xbc_piece(k) for k in range(SSD_XBC // COL_CHUNK)] + [dt_piece()]
    z_tasks = [z_piece(k) for k in range(SSD_INNER // COL_CHUNK)]
    pending = None
    for mm, vpu in head:
        mm()
        if pending is not None:
            if z_tasks:
                z_tasks.pop(0)()
            pending()
        pending = vpu
    pending()

    n_chunks = tm // c
    fillers = [(0, t) for t in z_tasks]
    for k in range(SC_DIM // COL_CHUNK):
        fillers.extend((0, t) for t in sc_piece(k))
    outs = [out_piece(i) for i in range(n_chunks)]
    fillers.extend((0, sc_half) for sc_half, _ in outs)
    per_chunk = len(ssd_chunk(0))
    fillers.extend((per_chunk * (i + 1), ssd_half) for i, (_, ssd_half) in enumerate(outs))
    stages = [s for ci in range(n_chunks) for s in ssd_chunk(ci)]
    n_fill = len(fillers)
    for i, stage in enumerate(stages):
        want = ((i + 1) * n_fill) // len(stages) - (n_fill - len(fillers))
        while want > 0 and fillers and fillers[0][0] <= i:
            fillers.pop(0)[1]()
            want -= 1
        stage()
    for _, task in fillers:
        task()


def _attn_kernel(sink_ref, x_ref, w_qkv_ref, b_qkv_ref, w_o_ref, b_o_ref, ones_ref, o_ref,
                 xb_s, q_s, kt_s, v_s, a_s, kprev_s, vprev_s):
    tm = x_ref.shape[0]
    blk = ATTN_BLOCK
    first_tile = pl.program_id(1) == 0
    nq = ATTN_HEADS * ATTN_HEAD_DIM
    nkv = ATTN_KV_HEADS * ATTN_HEAD_DIM
    nkv2 = ATTN_KV_HEADS * LANES
    half = ATTN_HEAD_DIM

    @pl.when(first_tile)
    def _():
        kprev_s[...] = jnp.zeros_like(kprev_s)
        vprev_s[...] = jnp.zeros_like(vprev_s)

    kt_s[:, :blk] = kprev_s[...]
    v_s[:blk, :] = vprev_s[...]
    xb_s[...] = x_ref[...].astype(BF16)

    def proj_piece(k):
        lo = k * COL_CHUNK
        st = {}

        def mm():
            st["v"] = _dot(xb_s[...], w_qkv_ref[:, lo:lo + COL_CHUNK]) + b_qkv_ref[:, lo:lo + COL_CHUNK]

        def q_vpu():
            q_s[:, lo:lo + COL_CHUNK] = (st.pop("v") * (ATTN_HEAD_DIM ** -0.5 * LOG2E)).astype(BF16)

        def k_vpu():
            kt = st.pop("v").T.astype(BF16)
            zero = jnp.zeros((half, tm), BF16)
            for h in range(ATTN_KV_HEADS):
                kh = kt[h * half:(h + 1) * half]
                for base, part in ((0, jnp.concatenate([kh, zero], axis=0)),
                                   (nkv2, jnp.concatenate([zero, kh], axis=0))):
                    kt_s[base + h * LANES:base + (h + 1) * LANES, blk:] = part
                    kprev_s[base + h * LANES:base + (h + 1) * LANES, :] = part[:, tm - blk:]

        def v_vpu():
            vd = st.pop("v")
            lo_c = lax.broadcasted_iota(jnp.int32, (tm, LANES), 1) < half
            for c2 in range(ATTN_KV_HEADS // 2):
                col = vd[:, c2 * LANES:(c2 + 1) * LANES]
                swapped = pltpu.roll(col, half, 1)
                planes = ((0, 2 * c2, jnp.where(lo_c, col, 0.0)), (nkv2, 2 * c2, jnp.where(lo_c, 0.0, swapped)),
                          (0, 2 * c2 + 1, jnp.where(lo_c, swapped, 0.0)), (nkv2, 2 * c2 + 1, jnp.where(lo_c, 0.0, col)))
                for base, h, part in planes:
                    part = part.astype(BF16)
                    v_s[blk:, base + h * LANES:base + (h + 1) * LANES] = part
                    vprev_s[:, base + h * LANES:base + (h + 1) * LANES] = part[tm - blk:]

        vpu = q_vpu if lo < nq else (k_vpu if lo < nq + nkv else v_vpu)
        return mm, vpu

    pending = None
    for k in range((nq + 2 * nkv) // COL_CHUNK):
        mm, vpu = proj_piece(k)
        mm()
        if pending is not None:
            pending()
        pending = vpu
    pending()

    qi = lax.broadcasted_iota(jnp.int32, (blk, 2 * blk), 0)
    kj = lax.broadcasted_iota(jnp.int32, (blk, 2 * blk), 1)
    band = (kj > qi) & (kj <= qi + blk)
    band0 = band & (kj >= jnp.where(first_tile, blk, 0))
    lane_o = lax.broadcasted_iota(jnp.int32, (blk, LANES), 1) < half

    def scores(n, j):
        c0 = (j // 2) * LANES
        keys = slice(n * blk, (n + 2) * blk)
        kk = jnp.concatenate([kt_s[c0:c0 + LANES, keys], kt_s[nkv2 + c0:nkv2 + c0 + LANES, keys]], axis=1)
        return _dot(q_s[n * blk:(n + 1) * blk, j * LANES:(j + 1) * LANES], kk)

    def probs(n, j, s):
        valid = band0 if n == 0 else band
        ps = []
        es = []
        for t in range(2):
            sink = sink_ref[2 * j + t] * LOG2E
            sh = jnp.where(valid, s[:, t * 2 * blk:(t + 1) * 2 * blk], -jnp.inf)
            m = jnp.maximum(jnp.max(sh, axis=-1, keepdims=True), sink)
            ps.append(jnp.exp2(sh - m).astype(BF16))
            es.append(jnp.exp2(sink - m))
        return jnp.concatenate(ps, axis=1), jnp.where(lane_o, es[0], es[1])

    def output(n, j, p, es):
        c0 = (j // 2) * LANES
        keys = slice(n * blk, (n + 2) * blk)
        vv = jnp.concatenate([v_s[keys, c0:c0 + LANES], v_s[keys, nkv2 + c0:nkv2 + c0 + LANES]], axis=0)
        r = _dot(p, jnp.concatenate([vv, ones_ref[...]], axis=1))
        a_s[n * blk:(n + 1) * blk, j * LANES:(j + 1) * LANES] = (r[:, :LANES] / (r[:, LANES:] + es)).astype(BF16)

    def out_block(n):
        rows = slice(n * blk, (n + 1) * blk)
        st = {}

        def mm():
            st["mix"] = _dot(a_s[rows, :], w_o_ref[...]) + b_o_ref[...]

        def residual():
            o_ref[rows, :] = ALPHA * x_ref[rows, :] + st.pop("mix")

        return mm, residual

    steps = [(n, j) for n in range(tm // blk) for j in range(ATTN_HEADS // 2)]
    pairs = ATTN_HEADS // 2
    extra = {}
    for n in range(tm // blk - 1):
        mm, norm = out_block(n)
        extra[pairs * (n + 1) + 3] = mm
        extra[pairs * (n + 1) + 6] = norm
    s_val = {}
    p_val = {}
    for i in range(len(steps) + 2):
        if i < len(steps):
            s_val[i] = scores(*steps[i])
        if i in extra:
            extra[i]()
        if 0 <= i - 1 < len(steps):
            p_val[i - 1] = probs(*steps[i - 1], s_val.pop(i - 1))
        if 0 <= i - 2 < len(steps):
            output(*steps[i - 2], *p_val.pop(i - 2))
    for task in out_block(tm // blk - 1):
        task()


def _ffn_kernel(r_ref, p_ref, gm_ref, bm_ref, wg_ref, wu_ref, wd_ref, g_ref, b_ref, wpg_ref, bpg_ref, wple_ref, o_ref,
                *x1_scratch, mixer_ln):
    tm = r_ref.shape[0]
    n_blk = tm // FFN_ROWS

    def blk(i):
        rows = slice(i * FFN_ROWS, (i + 1) * FFN_ROWS)
        st = {}

        def mixer_norm():
            x1 = r_ref[rows, :]
            if mixer_ln:
                x1 = _layer_norm(x1, gm_ref[...], bm_ref[...])
                x1_scratch[0][rows, :] = x1
            st["xb"] = x1.astype(BF16)

        def gate_up():
            xb = st.pop("xb")
            st["g"] = _dot(xb, wg_ref[...])
            st["u"] = _dot(xb, wu_ref[...])

        def act():
            st["h"] = (_silu(st.pop("g")) * st.pop("u")).astype(BF16)

        def down():
            st["y"] = _dot(st.pop("h"), wd_ref[...])

        def norm():
            x1 = x1_scratch[0][rows, :] if mixer_ln else r_ref[rows, :]
            st["x2"] = _layer_norm(ALPHA * x1 + st.pop("y"), g_ref[...], b_ref[...])

        def ple():
            st["gate"] = _dot(st["x2"].astype(BF16), wpg_ref[...])
            st["pe"] = _dot(p_ref[rows, :].astype(BF16), wple_ref[...])

        def out():
            o_ref[rows, :] = st.pop("x2") + st.pop("pe") * _sigmoid(st.pop("gate") + bpg_ref[...])

        return [mixer_norm, gate_up, act, down, norm, ple, out]

    stages = [blk(i) for i in range(n_blk)]
    n_st = len(stages[0])
    for step in range(n_st + 2 * (n_blk - 1)):
        for i in range(n_blk):
            k = step - 2 * i
            if 0 <= k < n_st:
                stages[i][k]()


def _with_casts(body, n_in, n_cast):
    def kernel_fn(*refs):
        ins, casts_in = refs[:n_in], refs[n_in:n_in + n_cast]
        out, casts_out = refs[n_in + n_cast], refs[n_in + n_cast + 1:n_in + 2 * n_cast + 1]
        scratch = refs[n_in + 2 * n_cast + 1:]
        body(*ins, out, *scratch)
        for src, dst in zip(casts_in, casts_out):
            dst[...] = src[...].astype(BF16)

    return kernel_fn


def _cast_plan(w, layer, steps, flat_step):
    _, rows, cols = w.shape
    per, reuse = rows // steps, 1
    while per % BF16_SUBLANES:
        per, reuse = per * 2, reuse * 2
    assert rows % per == 0
    in_spec = pl.BlockSpec((None, per, cols), lambda *g: (layer, flat_step(*g) // reuse, 0))
    out_spec = pl.BlockSpec((per, cols), lambda *g: (flat_step(*g) // reuse, 0))
    return in_spec, out_spec, jax.ShapeDtypeStruct((rows, cols), BF16)


def _resident(shape):
    return pl.BlockSpec(shape, lambda *_: (0,) * len(shape), pipeline_mode=pl.Buffered(1))


def _row(v):
    return v.reshape(1, -1).astype(F32)


def _pad_lanes(v, width):
    return jnp.pad(v, ((0, 0), (0, width - v.shape[1])))


def _mixer0(x2d, batch, w_in, conv_w, conv_b, dt_bias, a_log, d_skip, norm_w, sc_conv_w, w_out, ln_g, ln_b, next_w):
    t = x2d.shape[0]
    n_l = t // batch // TM_MIX0
    w_in = w_in.astype(BF16)
    dtb = _pad_lanes(jnp.tile(_row(dt_bias), (1, DT_REP)), LANES)
    alog = _pad_lanes(jnp.tile(_row(a_log), (1, DT_REP)), LANES)
    dexp = jnp.repeat(_row(d_skip), SSD_HEAD_DIM, axis=1)
    r = jnp.arange(LANES)[:, None]
    col = jnp.arange(DT_REP * SSD_INNER)[None, :]
    e1 = ((r < DT_REP * SSD_HEADS) & (col // SSD_HEAD_DIM == r)).astype(BF16)
    e2 = jnp.concatenate([e1, e1], axis=0)
    tri = (jnp.arange(SSD_CHUNK)[:, None] >= jnp.arange(SSD_CHUNK)[None, :]).astype(BF16)
    tril3 = jnp.concatenate([tri, tri, tri], axis=1)

    tok = lambda width: pl.BlockSpec((TM_MIX0, width), lambda b, l: (b * n_l + l, 0))
    plans = [_cast_plan(w, layer, batch * n_l, lambda b, l: b * n_l + l) for w, layer in next_w]
    in_specs = [tok(D_MODEL), _resident(w_in.shape), _resident((SSD_CONV, SSD_XBC)),
                _resident((1, SSD_XBC)), _resident((1, LANES)), _resident((1, LANES)),
                _resident((1, SSD_INNER)), _resident((1, SSD_INNER)), _resident((SC_CONV, SC_DIM)),
                _resident((MIX_OUT, D_MODEL)), _resident((1, D_MODEL)), _resident((1, D_MODEL)),
                _resident(e2.shape), _resident(tril3.shape)]
    outs = pl.pallas_call(
        _with_casts(_mixer0_kernel, len(in_specs), len(plans)),
        grid=(batch, n_l),
        in_specs=in_specs + [p[0] for p in plans],
        out_specs=[tok(D_MODEL)] + [p[1] for p in plans],
        out_shape=[jax.ShapeDtypeStruct((t, D_MODEL), F32)] + [p[2] for p in plans],
        scratch_shapes=[pltpu.VMEM((TM_MIX0, D_MODEL), BF16),
                        pltpu.VMEM((TM_MIX0, SSD_INNER), F32), pltpu.VMEM((TM_MIX0, SSD_XBC), F32),
                        pltpu.VMEM((TM_MIX0, LANES), F32), pltpu.VMEM((SSD_CHUNK, TM_MIX0), F32),
                        pltpu.VMEM((TM_MIX0, MIX_OUT), BF16),
                        pltpu.VMEM((SUBLANES, SSD_XBC), F32), pltpu.VMEM((SUBLANES, SC_DIM), F32),
                        pltpu.VMEM((SSD_GROUPS, SSD_STATE, SSD_HPG * SSD_HEAD_DIM), F32),
                        pltpu.VMEM((D_MODEL, 3 * SC_DIM), BF16), pltpu.VMEM((D_MODEL, LANES), BF16)],
        compiler_params=pltpu.CompilerParams(dimension_semantics=("arbitrary", "arbitrary"),
                                             vmem_limit_bytes=VMEM_LIMIT_BYTES),
        name="mixer0_ssd_shortconv",
    )(x2d, w_in, conv_w.astype(F32), _row(conv_b), dtb, alog, dexp, _row(norm_w), sc_conv_w.astype(F32),
      w_out.astype(BF16), _row(ln_g), _row(ln_b), e2, tril3, *[w for w, _ in next_w])
    return outs[0], outs[1:]


def _attn(x2d, batch, w_qkv, b_qkv, sinks, w_o, b_o, next_w):
    t = x2d.shape[0]
    n_l = t // batch // TM_ATTN
    nq = ATTN_HEADS * ATTN_HEAD_DIM
    nkv = ATTN_KV_HEADS * ATTN_HEAD_DIM
    width = nq + 2 * nkv
    planes = 2 * ATTN_KV_HEADS * LANES
    ones_blk = (jnp.arange(4 * ATTN_BLOCK)[:, None] // (2 * ATTN_BLOCK) == jnp.arange(LANES)[None, :] // ATTN_HEAD_DIM
                ).astype(BF16)
    tok = pl.BlockSpec((TM_ATTN, D_MODEL), lambda b, l: (b * n_l + l, 0))
    plans = [_cast_plan(w, layer, batch * n_l, lambda b, l: b * n_l + l) for w, layer in next_w]
    in_specs = [pl.BlockSpec(memory_space=pltpu.SMEM), tok, _resident((D_MODEL, width)), _resident((1, width)),
                _resident((nq, D_MODEL)), _resident((1, D_MODEL)), _resident(ones_blk.shape)]
    outs = pl.pallas_call(
        _with_casts(_attn_kernel, len(in_specs), len(plans)),
        grid=(batch, n_l),
        in_specs=in_specs + [p[0] for p in plans],
        out_specs=[tok] + [p[1] for p in plans],
        out_shape=[jax.ShapeDtypeStruct((t, D_MODEL), F32)] + [p[2] for p in plans],
        scratch_shapes=[pltpu.VMEM((TM_ATTN, D_MODEL), BF16),
                        pltpu.VMEM((TM_ATTN, nq), BF16), pltpu.VMEM((planes, TM_ATTN + ATTN_BLOCK), BF16),
                        pltpu.VMEM((TM_ATTN + ATTN_BLOCK, planes), BF16), pltpu.VMEM((TM_ATTN, nq), BF16),
                        pltpu.VMEM((planes, ATTN_BLOCK), BF16), pltpu.VMEM((ATTN_BLOCK, planes), BF16)],
        compiler_params=pltpu.CompilerParams(dimension_semantics=("arbitrary", "arbitrary"),
                                             vmem_limit_bytes=VMEM_LIMIT_BYTES),
        name="mixer1_swa",
    )(sinks.astype(F32), x2d, w_qkv, _row(b_qkv), w_o, _row(b_o), ones_blk,
      *[w for w, _ in next_w])
    return outs[0], outs[1:]


def _ffn(r2d, layer, mixer_ln, p3d, ln_mix_g, ln_mix_b, w_gate, w_up, w_down, ln_g, ln_b, w_ple_gate, b_ple_gate, w_ple,
         next_w):
    t = r2d.shape[0]
    steps = t // TM_FFN
    tok = lambda width: pl.BlockSpec((TM_FFN, width), lambda i: (i, 0))
    plans = [_cast_plan(w, lyr, steps, lambda i: i) for w, lyr in next_w]
    in_specs = [tok(D_MODEL), pl.BlockSpec((None, TM_FFN, PLE_DIM), lambda i: (layer, i, 0)),
                _resident((1, D_MODEL)), _resident((1, D_MODEL)),
                _resident((D_MODEL, D_FF)), _resident((D_MODEL, D_FF)), _resident((D_FF, D_MODEL)),
                _resident((1, D_MODEL)), _resident((1, D_MODEL)),
                _resident((D_MODEL, D_MODEL)), _resident((1, D_MODEL)), _resident((PLE_DIM, D_MODEL))]
    outs = pl.pallas_call(
        _with_casts(functools.partial(_ffn_kernel, mixer_ln=mixer_ln), len(in_specs), len(plans)),
        grid=(steps,),
        in_specs=in_specs + [p[0] for p in plans],
        out_specs=[tok(D_MODEL)] + [p[1] for p in plans],
        out_shape=[jax.ShapeDtypeStruct((t, D_MODEL), F32)] + [p[2] for p in plans],
        scratch_shapes=[pltpu.VMEM((TM_FFN, D_MODEL), F32)] if mixer_ln else [],
        compiler_params=pltpu.CompilerParams(dimension_semantics=("arbitrary",),
                                             vmem_limit_bytes=VMEM_LIMIT_BYTES),
        name="ffn_ple",
    )(r2d, p3d, _row(ln_mix_g[layer]), _row(ln_mix_b[layer]), w_gate, w_up, w_down,
      _row(ln_g[layer]), _row(ln_b[layer]), w_ple_gate, _row(b_ple_gate[layer]), w_ple, *[w for w, _ in next_w])
    return outs[0], outs[1:]


def kernel(x, p, w_in_mix, ssd_conv_w, ssd_conv_b, ssd_dt_bias, ssd_a_log, ssd_d, ssd_norm_w, sc_conv_w, w_out_mix, w_qkv, b_qkv, attn_sinks, w_o, b_o, ln_mix_g, ln_mix_b, w_ffn_gate, w_ffn_up, w_ffn_down, ln_ffn_g, ln_ffn_b, w_ple, w_ple_gate, b_ple_gate):
    batch, seq, d = x.shape
    assert d == D_MODEL and seq % TM_MIX0 == 0 and seq % TM_ATTN == 0 and (batch * seq) % TM_FFN == 0
    assert p.shape == (DEPTH, batch, seq, PLE_DIM)
    h = x.reshape(batch * seq, d)
    p2d = p.reshape(DEPTH, batch * seq, PLE_DIM)

    def ffn_weights(i):
        return [(w, i) for w in (w_ffn_gate, w_ffn_up, w_ffn_down, w_ple_gate, w_ple)]

    h, ffn0_w = _mixer0(h, batch, w_in_mix[0], ssd_conv_w[0], ssd_conv_b[0], ssd_dt_bias[0], ssd_a_log[0], ssd_d[0],
                        ssd_norm_w[0], sc_conv_w[0], w_out_mix[0], ln_mix_g[0], ln_mix_b[0], ffn_weights(0))
    h, attn_w = _ffn(h, 0, False, p2d, ln_mix_g, ln_mix_b, *ffn0_w[:3], ln_ffn_g, ln_ffn_b, ffn0_w[3], b_ple_gate,
                     ffn0_w[4], [(w_qkv, 0), (w_o, 0)])
    r, ffn1_w = _attn(h, batch, attn_w[0], b_qkv[0], attn_sinks[0], attn_w[1], b_o[0], ffn_weights(1))
    h, _ = _ffn(r, 1, True, p2d, ln_mix_g, ln_mix_b, *ffn1_w[:3], ln_ffn_g, ln_ffn_b, ffn1_w[3], b_ple_gate,
                ffn1_w[4], [])
    return h.reshape(batch, seq, d)
```

```python
import functools

import jax
import jax.numpy as jnp
from jax import lax
from jax.experimental import pallas as pl
from jax.experimental.pallas import tpu as pltpu

F32 = jnp.float32
BF16 = jnp.bfloat16

D_MODEL = 1024
DEPTH = 2
PLE_DIM = 256
SSD_HEADS = 16
SSD_HEAD_DIM = 64
SSD_INNER = SSD_HEADS * SSD_HEAD_DIM
SSD_GROUPS = 2
SSD_HPG = SSD_HEADS // SSD_GROUPS
SSD_STATE = 128
SSD_CONV = 4
SSD_CHUNK = 128
SSD_XBC = SSD_INNER + 2 * SSD_GROUPS * SSD_STATE
SC_DIM = 1024
SC_CONV = 3
MIX_OUT = SSD_INNER + SC_DIM
ATTN_HEADS = 16
ATTN_KV_HEADS = 4
ATTN_HEAD_DIM = 64
ATTN_BLOCK = 128
D_FF = ((8 * D_MODEL + 3 * 256 - 1) // (3 * 256)) * 256
ALPHA = (2 * DEPTH) ** 0.25
LN_EPS = 1e-5
LOG2E = 1.4426950408889634
RMS_EPS = 1e-5

LANES = 128
SUBLANES = 8
BF16_SUBLANES = 16
V7X_VMEM_BYTES = 64 * 1024 * 1024
VMEM_LIMIT_BYTES = V7X_VMEM_BYTES - 8 * 1024 * 1024

TM_MIX0 = 512
TM_ATTN = 1024
TM_FFN = 1024
FFN_ROWS = 256
FFN_SPLIT = 1536

_C_Z = 0
_C_XBC = _C_Z + SSD_INNER
_C_SCB = 0
_C_SCC = _C_SCB + SC_DIM
_C_SCH = _C_SCC + SC_DIM
_W_DT = SSD_INNER + SSD_XBC
_W_SC = _W_DT + SSD_HEADS
DT_REP = 2
COL_CHUNK = 256


def _dot(a, b):
    return jnp.dot(a, b, preferred_element_type=F32)


def _dot_nt(a, b):
    return lax.dot_general(a, b, (((1,), (1,)), ((), ())), preferred_element_type=F32)


def _dot_tn(a, b):
    return lax.dot_general(a, b, (((0,), (0,)), ((), ())), preferred_element_type=F32)


def _sigmoid(x):
    return 1.0 / (1.0 + jnp.exp(-x))


def _silu(x):
    return x * _sigmoid(x)


def _layer_norm(r, g, b):
    mu = jnp.mean(r, axis=-1, keepdims=True)
    xc = r - mu
    var = jnp.mean(xc * xc, axis=-1, keepdims=True)
    return xc * lax.rsqrt(var + LN_EPS) * g + b


def _split_bf16(v, parts):
    out = []
    rem = v
    for i in range(parts):
        p = rem.astype(BF16)
        out.append(p)
        if i + 1 < parts:
            rem = rem - p.astype(F32)
    return out


def _causal_conv(cur, prev_tail, w):
    width, cols = w.shape
    rows = lax.broadcasted_iota(jnp.int32, (SUBLANES, cols), 0)
    acc = cur * w[width - 1:width, :]
    for s in range(1, width):
        rolled = pltpu.roll(cur, s, 0)
        head = jnp.where(rows < s, pltpu.roll(prev_tail, s, 0), rolled[0:SUBLANES])
        shifted = jnp.concatenate([head, rolled[SUBLANES:]], axis=0)
        acc = acc + shifted * w[width - 1 - s:width - s, :]
    return acc


def _mixer0_kernel(x_ref, w_in_ref, cw_ref, cb_ref, dtb_ref, alog_ref, dexp_ref, nw_ref, scw_ref, w_out_ref,
                   g_ref, b_ref, e2_ref, tril_ref, o_ref,
                   xb_s, z_s, xbc_s, dt_s, acs_s, ycat_s, prev_xbc_s, prev_sc_s, h_s, w_sc_s, w_dt_s):
    tm = x_ref.shape[0]
    c = SSD_CHUNK
    gw = SSD_HPG * SSD_HEAD_DIM

    @pl.when(pl.program_id(1) == 0)
    def _():
        prev_xbc_s[...] = jnp.zeros_like(prev_xbc_s)
        prev_sc_s[...] = jnp.zeros_like(prev_sc_s)
        h_s[...] = jnp.zeros_like(h_s)

    @pl.when(jnp.logical_and(pl.program_id(0) == 0, pl.program_id(1) == 0))
    def _():
        for k in range(3 * SC_DIM // COL_CHUNK):
            cols = slice(k * COL_CHUNK, (k + 1) * COL_CHUNK)
            w_sc_s[:, cols] = w_in_ref[:, _W_SC + k * COL_CHUNK:_W_SC + (k + 1) * COL_CHUNK]
        blk = w_in_ref[:, _W_DT:_W_DT + LANES].astype(F32)
        lane = lax.broadcasted_iota(jnp.int32, blk.shape, 1)
        rep = jnp.where(lane < DT_REP * SSD_HEADS, pltpu.roll(blk, SSD_HEADS, 1), 0.0)
        w_dt_s[...] = jnp.where(lane < SSD_HEADS, blk, rep).astype(BF16)

    xb_s[...] = x_ref[...].astype(BF16)

    def proj(w_ref, lo, width):
        return _dot(xb_s[...], w_ref[:, lo:lo + width])

    def xbc_piece(k):
        cols = slice(k * COL_CHUNK, (k + 1) * COL_CHUNK)
        st = {}

        def mm():
            st["raw"] = proj(w_in_ref, _C_XBC + k * COL_CHUNK, COL_CHUNK)

        def vpu():
            raw = st.pop("raw")
            xbc_s[:, cols] = _silu(_causal_conv(raw, prev_xbc_s[:, cols], cw_ref[:, cols]) + cb_ref[:, cols])
            prev_xbc_s[:, cols] = raw[tm - SUBLANES:tm]

        return mm, vpu

    def dt_piece():
        st = {}

        def mm():
            st["raw"] = proj(w_dt_s, 0, LANES)

        def vpu():
            pre = st.pop("raw") + dtb_ref[...]
            dt = jnp.maximum(pre, 0.0) + jnp.log1p(jnp.exp(-jnp.abs(pre)))
            dt_s[...] = dt
            lane_row = lax.broadcasted_iota(jnp.int32, (1, LANES), 1)
            da = dt * jnp.where(lane_row < DT_REP * SSD_HEADS, -jnp.exp(alog_ref[...]), 0.0)
            da = jnp.concatenate([da[i * c:(i + 1) * c] for i in range(tm // c)], axis=1)
            acs_s[...] = _dot(tril_ref[...], jnp.concatenate(_split_bf16(da, 3), axis=0))

        return mm, vpu

    def z_piece(k):
        def mm():
            z_s[:, k * COL_CHUNK:(k + 1) * COL_CHUNK] = proj(w_in_ref, _C_Z + k * COL_CHUNK, COL_CHUNK)

        return mm

    def sc_piece(k):
        cols = slice(k * COL_CHUNK, (k + 1) * COL_CHUNK)
        st = {}

        def mm_b():
            st["b"] = proj(w_sc_s, _C_SCB + k * COL_CHUNK, COL_CHUNK)

        def mm_c():
            st["c"] = proj(w_sc_s, _C_SCC + k * COL_CHUNK, COL_CHUNK)

        def mm_h():
            st["u"] = st.pop("c") * proj(w_sc_s, _C_SCH + k * COL_CHUNK, COL_CHUNK)

        def vpu():
            u = st.pop("u")
            y_sc = st.pop("b") * _causal_conv(u, prev_sc_s[:, cols], scw_ref[:, cols])
            ycat_s[:, SSD_INNER + k * COL_CHUNK:SSD_INNER + (k + 1) * COL_CHUNK] = y_sc.astype(BF16)
            prev_sc_s[:, cols] = u[tm - SUBLANES:tm]

        return mm_b, mm_c, mm_h, vpu

    def out_piece(i):
        rows = slice(i * c, (i + 1) * c)
        st = {}

        def sc_half():
            st["mix"] = _dot(ycat_s[rows, SSD_INNER:], w_out_ref[SSD_INNER:, :])

        def ssd_half():
            mix = st.pop("mix") + _dot(ycat_s[rows, :SSD_INNER], w_out_ref[:SSD_INNER, :])
            o_ref[rows, :] = _layer_norm(ALPHA * x_ref[rows, :] + mix, g_ref[...], b_ref[...])

        return sc_half, ssd_half

    def ssd_chunk(ci):
        rows = slice(ci * c, (ci + 1) * c)
        st = {}

        def s1():
            lane_sq = lax.broadcasted_iota(jnp.int32, (c, LANES), 1)
            dt = dt_s[rows, :]
            a_cs = acs_s[:, ci * LANES:(ci + 1) * LANES]
            ecs = jnp.exp(a_cs)
            dtdte = dt * jnp.exp(a_cs[c - 1:c, :] - a_cs)
            q = jnp.where(lane_sq < SSD_HEADS, dtdte, ecs)
            ex = _dot(jnp.concatenate(_split_bf16(q, 2), axis=1), e2_ref[...])
            st["dtdte_e"] = ex[:, :SSD_INNER]
            st["ecs_e"] = ex[:, SSD_INNER:]
            st["a_cs"] = a_cs
            st["t"] = jnp.where(lane_sq < SSD_HEADS, a_cs, dt).T

        def s2():
            group_lane = lax.broadcasted_iota(jnp.int32, (c, SSD_GROUPS * SSD_STATE), 1) // SSD_STATE
            bmf = xbc_s[rows, SSD_INNER:SSD_INNER + SSD_GROUPS * SSD_STATE]
            bm = bmf.astype(BF16)
            cm = xbc_s[rows, SSD_INNER + SSD_GROUPS * SSD_STATE:].astype(BF16)
            st["bm"] = bm
            st["cm"] = cm
            b_diag = jnp.concatenate([jnp.where(group_lane == g, bmf, 0.0) for g in range(SSD_GROUPS)], axis=0)
            st["cb"] = _dot_nt(cm, b_diag.astype(BF16))
            st["y"] = []

        def s2_quad(qi):
            lane_sq = lax.broadcasted_iota(jnp.int32, (c, LANES), 1)
            causal = lax.broadcasted_iota(jnp.int32, (c, LANES), 0) >= lane_sq
            quad_lane = lax.broadcasted_iota(jnp.int32, (c, 4 * SSD_HEAD_DIM), 1) // SSD_HEAD_DIM
            g = (4 * qi) // SSD_HPG
            h0 = 4 * qi
            a_cs = st["a_cs"]
            t = st["t"]
            cb = st["cb"][:, g * c:(g + 1) * c]
            ms = []
            for h in range(h0, h0 + 4):
                seg = jnp.broadcast_to(a_cs[:, h:h + 1], (c, LANES)) - t[h:h + 1, :]
                dec = jnp.exp(jnp.where(causal, seg, -jnp.inf))
                ms.append((cb * dec * t[SSD_HEADS + h:SSD_HEADS + h + 1, :]).astype(BF16))
            xq = xbc_s[rows, h0 * SSD_HEAD_DIM:(h0 + 4) * SSD_HEAD_DIM]
            rhs = jnp.concatenate([jnp.where(quad_lane == k, xq, 0.0) for k in range(4)], axis=0)
            st["y"].append(_dot(jnp.concatenate(ms, axis=1), rhs.astype(BF16)))

        def s3():
            for key in ("a_cs", "t", "cb"):
                st.pop(key)
            bm = st.pop("bm")
            cm = st.pop("cm")
            ecs_e = st["ecs_e"]
            dtdte_e = st.pop("dtdte_e")
            y_off = []
            for g in range(SSD_GROUPS):
                hg = h_s[g]
                y_off.append(_dot(cm[:, g * SSD_STATE:(g + 1) * SSD_STATE], hg.astype(BF16)))
                xd = (xbc_s[rows, g * gw:(g + 1) * gw] * dtdte_e[:, g * gw:(g + 1) * gw]).astype(BF16)
                new = _dot_tn(bm[:, g * SSD_STATE:(g + 1) * SSD_STATE], xd)
                h_s[g] = hg * ecs_e[c - 1:c, g * gw:(g + 1) * gw] + new
            st["y_off"] = jnp.concatenate(y_off, axis=1)

        def s4():
            y = jnp.concatenate(st.pop("y"), axis=1) + st.pop("y_off") * st.pop("ecs_e")
            y = y + xbc_s[rows, :SSD_INNER] * dexp_ref[...]
            y = y * _silu(z_s[rows, :])
            yn = []
            for g in range(SSD_GROUPS):
                yg = y[:, g * gw:(g + 1) * gw]
                yn.append(yg * lax.rsqrt(jnp.mean(yg * yg, axis=-1, keepdims=True) + RMS_EPS))
            ycat_s[rows, 0:SSD_INNER] = (jnp.concatenate(yn, axis=1) * nw_ref[...]).astype(BF16)

        quads = [functools.partial(s2_quad, qi) for qi in range(SSD_HEADS // 4)]
        return [s1, s2] + quads + [s3, s4]

    head = [xbc_piece(k) for k in range(SSD_XBC // COL_CHUNK)] + [dt_piece()]
    z_tasks = [z_piece(k) for k in range(SSD_INNER // COL_CHUNK)]
    pending = None
    for mm, vpu in head:
        mm()
        if pending is not None:
            if z_tasks:
                z_tasks.pop(0)()
            pending()
        pending = vpu
    pending()

    n_chunks = tm // c
    fillers = [(0, t) for t in z_tasks]
    for k in range(SC_DIM // COL_CHUNK):
        fillers.extend((0, t) for t in sc_piece(k))
    outs = [out_piece(i) for i in range(n_chunks)]
    fillers.extend((0, sc_half) for sc_half, _ in outs)
    per_chunk = len(ssd_chunk(0))
    fillers.extend((per_chunk * (i + 1), ssd_half) for i, (_, ssd_half) in enumerate(outs))
    stages = [s for ci in range(n_chunks) for s in ssd_chunk(ci)]
    n_fill = len(fillers)
    for i, stage in enumerate(stages):
        want = ((i + 1) * n_fill) // len(stages) - (n_fill - len(fillers))
        while want > 0 and fillers and fillers[0][0] <= i:
            fillers.pop(0)[1]()
            want -= 1
        stage()
    for _, task in fillers:
        task()


def _attn_kernel(sink_ref, x_ref, w_qkv_ref, b_qkv_ref, w_o_ref, b_o_ref, ones_ref, o_ref,
                 xb_s, q_s, kt_s, v_s, a_s, kprev_s, vprev_s):
    tm = x_ref.shape[0]
    blk = ATTN_BLOCK
    first_tile = pl.program_id(1) == 0
    nq = ATTN_HEADS * ATTN_HEAD_DIM
    nkv = ATTN_KV_HEADS * ATTN_HEAD_DIM
    nkv2 = ATTN_KV_HEADS * LANES
    half = ATTN_HEAD_DIM

    @pl.when(first_tile)
    def _():
        kprev_s[...] = jnp.zeros_like(kprev_s)
        vprev_s[...] = jnp.zeros_like(vprev_s)

    kt_s[:, :blk] = kprev_s[...]
    v_s[:blk, :] = vprev_s[...]
    xb_s[...] = x_ref[...].astype(BF16)

    def proj_piece(k):
        lo = k * COL_CHUNK
        st = {}

        def mm():
            st["v"] = _dot(xb_s[...], w_qkv_ref[:, lo:lo + COL_CHUNK]) + b_qkv_ref[:, lo:lo + COL_CHUNK]

        def q_vpu():
            q_s[:, lo:lo + COL_CHUNK] = (st.pop("v") * (ATTN_HEAD_DIM ** -0.5 * LOG2E)).astype(BF16)

        def k_vpu():
            kt = st.pop("v").T.astype(BF16)
            zero = jnp.zeros((half, tm), BF16)
            for h in range(ATTN_KV_HEADS):
                kh = kt[h * half:(h + 1) * half]
                for base, part in ((0, jnp.concatenate([kh, zero], axis=0)),
                                   (nkv2, jnp.concatenate([zero, kh], axis=0))):
                    kt_s[base + h * LANES:base + (h + 1) * LANES, blk:] = part
                    kprev_s[base + h * LANES:base + (h + 1) * LANES, :] = part[:, tm - blk:]

        def v_vpu():
            vd = st.pop("v")
            lo_c = lax.broadcasted_iota(jnp.int32, (tm, LANES), 1) < half
            for c2 in range(ATTN_KV_HEADS // 2):
                col = vd[:, c2 * LANES:(c2 + 1) * LANES]
                swapped = pltpu.roll(col, half, 1)
                planes = ((0, 2 * c2, jnp.where(lo_c, col, 0.0)), (nkv2, 2 * c2, jnp.where(lo_c, 0.0, swapped)),
                          (0, 2 * c2 + 1, jnp.where(lo_c, swapped, 0.0)), (nkv2, 2 * c2 + 1, jnp.where(lo_c, 0.0, col)))
                for base, h, part in planes:
                    part = part.astype(BF16)
                    v_s[blk:, base + h * LANES:base + (h + 1) * LANES] = part
                    vprev_s[:, base + h * LANES:base + (h + 1) * LANES] = part[tm - blk:]

        vpu = q_vpu if lo < nq else (k_vpu if lo < nq + nkv else v_vpu)
        return mm, vpu

    pending = None
    for k in range((nq + 2 * nkv) // COL_CHUNK):
        mm, vpu = proj_piece(k)
        mm()
        if pending is not None:
            pending()
        pending = vpu
    pending()

    qi = lax.broadcasted_iota(jnp.int32, (blk, 2 * blk), 0)
    kj = lax.broadcasted_iota(jnp.int32, (blk, 2 * blk), 1)
    band = (kj > qi) & (kj <= qi + blk)
    band0 = band & (kj >= jnp.where(first_tile, blk, 0))
    lane_o = lax.broadcasted_iota(jnp.int32, (blk, LANES), 1) < half

    def scores(n, j):
        c0 = (j // 2) * LANES
        keys = slice(n * blk, (n + 2) * blk)
        kk = jnp.concatenate([kt_s[c0:c0 + LANES, keys], kt_s[nkv2 + c0:nkv2 + c0 + LANES, keys]], axis=1)
        return _dot(q_s[n * blk:(n + 1) * blk, j * LANES:(j + 1) * LANES], kk)

    def probs(n, j, s):
        valid = band0 if n == 0 else band
        ps = []
        es = []
        for t in range(2):
            sink = sink_ref[2 * j + t] * LOG2E
            sh = jnp.where(valid, s[:, t * 2 * blk:(t + 1) * 2 * blk], -jnp.inf)
            m = jnp.maximum(jnp.max(sh, axis=-1, keepdims=True), sink)
            ps.append(jnp.exp2(sh - m).astype(BF16))
            es.append(jnp.exp2(sink - m))
        return jnp.concatenate(ps, axis=1), jnp.where(lane_o, es[0], es[1])

    def output(n, j, p, es):
        c0 = (j // 2) * LANES
        keys = slice(n * blk, (n + 2) * blk)
        vv = jnp.concatenate([v_s[keys, c0:c0 + LANES], v_s[keys, nkv2 + c0:nkv2 + c0 + LANES]], axis=0)
        r = _dot(p, jnp.concatenate([vv, ones_ref[...]], axis=1))
        a_s[n * blk:(n + 1) * blk, j * LANES:(j + 1) * LANES] = (r[:, :LANES] / (r[:, LANES:] + es)).astype(BF16)

    def out_block(n):
        rows = slice(n * blk, (n + 1) * blk)
        st = {}

        def mm():
            st["mix"] = _dot(a_s[rows, :], w_o_ref[...]) + b_o_ref[...]

        def residual():
            o_ref[rows, :] = ALPHA * x_ref[rows, :] + st.pop("mix")

        return mm, residual

    steps = [(n, j) for n in range(tm // blk) for j in range(ATTN_HEADS // 2)]
    pairs = ATTN_HEADS // 2
    extra = {}
    for n in range(tm // blk - 1):
        mm, norm = out_block(n)
        extra[pairs * (n + 1) + 3] = mm
        extra[pairs * (n + 1) + 6] = norm
    s_val = {}
    p_val = {}
    for i in range(len(steps) + 2):
        if i < len(steps):
            s_val[i] = scores(*steps[i])
        if i in extra:
            extra[i]()
        if 0 <= i - 1 < len(steps):
            p_val[i - 1] = probs(*steps[i - 1], s_val.pop(i - 1))
        if 0 <= i - 2 < len(steps):
            output(*steps[i - 2], *p_val.pop(i - 2))
    for task in out_block(tm // blk - 1):
        task()


def _ffn_kernel(r_ref, p_ref, gm_ref, bm_ref, wg_ref, wu_ref, wd_ref, g_ref, b_ref, wpg_ref, bpg_ref, wple_ref, o_ref,
                *x1_scratch, mixer_ln):
    tm = r_ref.shape[0]
    n_blk = tm // FFN_ROWS

    def blk(i):
        rows = slice(i * FFN_ROWS, (i + 1) * FFN_ROWS)
        st = {}

        def mixer_norm():
            x1 = r_ref[rows, :]
            if mixer_ln:
                x1 = _layer_norm(x1, gm_ref[...], bm_ref[...])
                x1_scratch[0][rows, :] = x1
            st["xb"] = x1.astype(BF16)

        def gate_up(part):
            cols = slice(0, FFN_SPLIT) if part == 0 else slice(FFN_SPLIT, D_FF)

            def stage():
                xb = st["xb"]
                st["g"] = _dot(xb, wg_ref[:, cols])
                st["u"] = _dot(xb, wu_ref[:, cols])

            return stage

        def act(part):
            def stage():
                st["h%d" % part] = (_silu(st.pop("g")) * st.pop("u")).astype(BF16)

            return stage

        def down():
            st.pop("xb")
            st["y"] = (_dot(st.pop("h0"), wd_ref[:FFN_SPLIT, :]) + _dot(st.pop("h1"), wd_ref[FFN_SPLIT:, :]))

        def norm():
            x1 = x1_scratch[0][rows, :] if mixer_ln else r_ref[rows, :]
            st["x2"] = _layer_norm(ALPHA * x1 + st.pop("y"), g_ref[...], b_ref[...])

        def ple():
            st["gate"] = _dot(st["x2"].astype(BF16), wpg_ref[...])
            st["pe"] = _dot(p_ref[rows, :].astype(BF16), wple_ref[...])

        def out():
            o_ref[rows, :] = st.pop("x2") + st.pop("pe") * _sigmoid(st.pop("gate") + bpg_ref[...])

        return [mixer_norm, gate_up(0), act(0), gate_up(1), act(1), down, norm, ple, out]

    stages = [blk(i) for i in range(n_blk)]
    n_st = len(stages[0])
    for step in range(n_st + 2 * (n_blk - 1)):
        for i in range(n_blk):
            k = step - 2 * i
            if 0 <= k < n_st:
                stages[i][k]()


def _with_casts(body, n_in, n_cast):
    def kernel_fn(*refs):
        ins, casts_in = refs[:n_in], refs[n_in:n_in + n_cast]
        out, casts_out = refs[n_in + n_cast], refs[n_in + n_cast + 1:n_in + 2 * n_cast + 1]
        scratch = refs[n_in + 2 * n_cast + 1:]
        body(*ins, out, *scratch)
        for src, dst in zip(casts_in, casts_out):
            dst[...] = src[...].astype(BF16)

    return kernel_fn


def _cast_plan(w, layer, steps, flat_step):
    _, rows, cols = w.shape
    per, reuse = rows // steps, 1
    while per % BF16_SUBLANES:
        per, reuse = per * 2, reuse * 2
    assert rows % per == 0
    in_spec = pl.BlockSpec((None, per, cols), lambda *g: (layer, flat_step(*g) // reuse, 0))
    out_spec = pl.BlockSpec((per, cols), lambda *g: (flat_step(*g) // reuse, 0))
    return in_spec, out_spec, jax.ShapeDtypeStruct((rows, cols), BF16)


def _resident(shape):
    return pl.BlockSpec(shape, lambda *_: (0,) * len(shape), pipeline_mode=pl.Buffered(1))


def _row(v):
    return v.reshape(1, -1).astype(F32)


def _pad_lanes(v, width):
    return jnp.pad(v, ((0, 0), (0, width - v.shape[1])))


def _mixer0(x2d, batch, w_in, conv_w, conv_b, dt_bias, a_log, d_skip, norm_w, sc_conv_w, w_out, ln_g, ln_b, next_w):
    t = x2d.shape[0]
    n_l = t // batch // TM_MIX0
    w_in = w_in.astype(BF16)
    dtb = _pad_lanes(jnp.tile(_row(dt_bias), (1, DT_REP)), LANES)
    alog = _pad_lanes(jnp.tile(_row(a_log), (1, DT_REP)), LANES)
    dexp = jnp.repeat(_row(d_skip), SSD_HEAD_DIM, axis=1)
    r = jnp.arange(LANES)[:, None]
    col = jnp.arange(DT_REP * SSD_INNER)[None, :]
    e1 = ((r < DT_REP * SSD_HEADS) & (col // SSD_HEAD_DIM == r)).astype(BF16)
    e2 = jnp.concatenate([e1, e1], axis=0)
    tri = (jnp.arange(SSD_CHUNK)[:, None] >= jnp.arange(SSD_CHUNK)[None, :]).astype(BF16)
    tril3 = jnp.concatenate([tri, tri, tri], axis=1)

    tok = lambda width: pl.BlockSpec((TM_MIX0, width), lambda b, l: (b * n_l + l, 0))
    plans = [_cast_plan(w, layer, batch * n_l, lambda b, l: b * n_l + l) for w, layer in next_w]
    in_specs = [tok(D_MODEL), _resident(w_in.shape), _resident((SSD_CONV, SSD_XBC)),
                _resident((1, SSD_XBC)), _resident((1, LANES)), _resident((1, LANES)),
                _resident((1, SSD_INNER)), _resident((1, SSD_INNER)), _resident((SC_CONV, SC_DIM)),
                _resident((MIX_OUT, D_MODEL)), _resident((1, D_MODEL)), _resident((1, D_MODEL)),
                _resident(e2.shape), _resident(tril3.shape)]
    outs = pl.pallas_call(
        _with_casts(_mixer0_kernel, len(in_specs), len(plans)),
        grid=(batch, n_l),
        in_specs=in_specs + [p[0] for p in plans],
        out_specs=[tok(D_MODEL)] + [p[1] for p in plans],
        out_shape=[jax.ShapeDtypeStruct((t, D_MODEL), F32)] + [p[2] for p in plans],
        scratch_shapes=[pltpu.VMEM((TM_MIX0, D_MODEL), BF16),
                        pltpu.VMEM((TM_MIX0, SSD_INNER), F32), pltpu.VMEM((TM_MIX0, SSD_XBC), F32),
                        pltpu.VMEM((TM_MIX0, LANES), F32), pltpu.VMEM((SSD_CHUNK, TM_MIX0), F32),
                        pltpu.VMEM((TM_MIX0, MIX_OUT), BF16),
                        pltpu.VMEM((SUBLANES, SSD_XBC), F32), pltpu.VMEM((SUBLANES, SC_DIM), F32),
                        pltpu.VMEM((SSD_GROUPS, SSD_STATE, SSD_HPG * SSD_HEAD_DIM), F32),
                        pltpu.VMEM((D_MODEL, 3 * SC_DIM), BF16), pltpu.VMEM((D_MODEL, LANES), BF16)],
        compiler_params=pltpu.CompilerParams(dimension_semantics=("arbitrary", "arbitrary"),
                                             vmem_limit_bytes=VMEM_LIMIT_BYTES),
        name="mixer0_ssd_shortconv",
    )(x2d, w_in, conv_w.astype(F32), _row(conv_b), dtb, alog, dexp, _row(norm_w), sc_conv_w.astype(F32),
      w_out.astype(BF16), _row(ln_g), _row(ln_b), e2, tril3, *[w for w, _ in next_w])
    return outs[0], outs[1:]


def _attn(x2d, batch, w_qkv, b_qkv, sinks, w_o, b_o, next_w):
    t = x2d.shape[0]
    n_l = t // batch // TM_ATTN
    nq = ATTN_HEADS * ATTN_HEAD_DIM
    nkv = ATTN_KV_HEADS * ATTN_HEAD_DIM
    width = nq + 2 * nkv
    planes = 2 * ATTN_KV_HEADS * LANES
    ones_blk = (jnp.arange(4 * ATTN_BLOCK)[:, None] // (2 * ATTN_BLOCK) == jnp.arange(LANES)[None, :] // ATTN_HEAD_DIM
                ).astype(BF16)
    tok = pl.BlockSpec((TM_ATTN, D_MODEL), lambda b, l: (b * n_l + l, 0))
    plans = [_cast_plan(w, layer, batch * n_l, lambda b, l: b * n_l + l) for w, layer in next_w]
    in_specs = [pl.BlockSpec(memory_space=pltpu.SMEM), tok, _resident((D_MODEL, width)), _resident((1, width)),
                _resident((nq, D_MODEL)), _resident((1, D_MODEL)), _resident(ones_blk.shape)]
    outs = pl.pallas_call(
        _with_casts(_attn_kernel, len(in_specs), len(plans)),
        grid=(batch, n_l),
        in_specs=in_specs + [p[0] for p in plans],
        out_specs=[tok] + [p[1] for p in plans],
        out_shape=[jax.ShapeDtypeStruct((t, D_MODEL), F32)] + [p[2] for p in plans],
        scratch_shapes=[pltpu.VMEM((TM_ATTN, D_MODEL), BF16),
                        pltpu.VMEM((TM_ATTN, nq), BF16), pltpu.VMEM((planes, TM_ATTN + ATTN_BLOCK), BF16),
                        pltpu.VMEM((TM_ATTN + ATTN_BLOCK, planes), BF16), pltpu.VMEM((TM_ATTN, nq), BF16),
                        pltpu.VMEM((planes, ATTN_BLOCK), BF16), pltpu.VMEM((ATTN_BLOCK, planes), BF16)],
        compiler_params=pltpu.CompilerParams(dimension_semantics=("arbitrary", "arbitrary"),
                                             vmem_limit_bytes=VMEM_LIMIT_BYTES),
        name="mixer1_swa",
    )(sinks.astype(F32), x2d, w_qkv, _row(b_qkv), w_o, _row(b_o), ones_blk,
      *[w for w, _ in next_w])
    return outs[0], outs[1:]


def _ffn(r2d, layer, mixer_ln, p3d, ln_mix_g, ln_mix_b, w_gate, w_up, w_down, ln_g, ln_b, w_ple_gate, b_ple_gate, w_ple,
         next_w):
    t = r2d.shape[0]
    steps = t // TM_FFN
    tok = lambda width: pl.BlockSpec((TM_FFN, width), lambda i: (i, 0))
    plans = [_cast_plan(w, lyr, steps, lambda i: i) for w, lyr in next_w]
    in_specs = [tok(D_MODEL), pl.BlockSpec((None, TM_FFN, PLE_DIM), lambda i: (layer, i, 0)),
                _resident((1, D_MODEL)), _resident((1, D_MODEL)),
                _resident((D_MODEL, D_FF)), _resident((D_MODEL, D_FF)), _resident((D_FF, D_MODEL)),
                _resident((1, D_MODEL)), _resident((1, D_MODEL)),
                _resident((D_MODEL, D_MODEL)), _resident((1, D_MODEL)), _resident((PLE_DIM, D_MODEL))]
    outs = pl.pallas_call(
        _with_casts(functools.partial(_ffn_kernel, mixer_ln=mixer_ln), len(in_specs), len(plans)),
        grid=(steps,),
        in_specs=in_specs + [p[0] for p in plans],
        out_specs=[tok(D_MODEL)] + [p[1] for p in plans],
        out_shape=[jax.ShapeDtypeStruct((t, D_MODEL), F32)] + [p[2] for p in plans],
        scratch_shapes=[pltpu.VMEM((TM_FFN, D_MODEL), F32)] if mixer_ln else [],
        compiler_params=pltpu.CompilerParams(dimension_semantics=("arbitrary",),
                                             vmem_limit_bytes=VMEM_LIMIT_BYTES),
        name="ffn_ple",
    )(r2d, p3d, _row(ln_mix_g[layer]), _row(ln_mix_b[layer]), w_gate, w_up, w_down,
      _row(ln_g[layer]), _row(ln_b[layer]), w_ple_gate, _row(b_ple_gate[layer]), w_ple, *[w for w, _ in next_w])
    return outs[0], outs[1:]


def kernel(x, p, w_in_mix, ssd_conv_w, ssd_conv_b, ssd_dt_bias, ssd_a_log, ssd_d, ssd_norm_w, sc_conv_w, w_out_mix, w_qkv, b_qkv, attn_sinks, w_o, b_o, ln_mix_g, ln_mix_b, w_ffn_gate, w_ffn_up, w_ffn_down, ln_ffn_g, ln_ffn_b, w_ple, w_ple_gate, b_ple_gate):
    batch, seq, d = x.shape
    assert d == D_MODEL and seq % TM_MIX0 == 0 and seq % TM_ATTN == 0 and (batch * seq) % TM_FFN == 0
    assert p.shape == (DEPTH, batch, seq, PLE_DIM)
    h = x.reshape(batch * seq, d)
    p2d = p.reshape(DEPTH, batch * seq, PLE_DIM)

    def ffn_weights(i):
        return [(w, i) for w in (w_ffn_gate, w_ffn_up, w_ffn_down, w_ple_gate, w_ple)]

    h, ffn0_w = _mixer0(h, batch, w_in_mix[0], ssd_conv_w[0], ssd_conv_b[0], ssd_dt_bias[0], ssd_a_log[0], ssd_d[0],
                        ssd_norm_w[0], sc_conv_w[0], w_out_mix[0], ln_mix_g[0], ln_mix_b[0], ffn_weights(0))
    h, attn_w = _ffn(h, 0, False, p2d, ln_mix_g, ln_mix_b, *ffn0_w[:3], ln_ffn_g, ln_ffn_b, ffn0_w[3], b_ple_gate,
                     ffn0_w[4], [(w_qkv, 0), (w_o, 0)])
    r, ffn1_w = _attn(h, batch, attn_w[0], b_qkv[0], attn_sinks[0], attn_w[1], b_o[0], ffn_weights(1))
    h, _ = _ffn(r, 1, True, p2d, ln_mix_g, ln_mix_b, *ffn1_w[:3], ln_ffn_g, ln_ffn_b, ffn1_w[3], b_ple_gate,
                ffn1_w[4], [])
    return h.reshape(batch, seq, d)
```
